```python
import jax, jax.numpy as jnp
from jax import lax
import numpy as np

D_MODEL = 4096
BATCH = 1
SEQ = 8192
DEPTH = 4

N_META = 16
N_MIXERS = 4
GROUP_W = D_MODEL // N_MIXERS
D_MIX = N_MIXERS * GROUP_W
HEAD_DIM = 128
N_ATTN_HEADS = GROUP_W // HEAD_DIM
BLOCK_Q = 128
CONF_K = 31
SC_K = 3
POOL_WINDOWS = (2, 4, 8, 16)
POOL_GW = GROUP_W // len(POOL_WINDOWS)
N_CHUNKS = 13
D_IN = N_CHUNKS * GROUP_W
EPS = 1e-6

kernel_name = "hymba_parallel_conformer_shortconv_stickbreak_pool"


def rmsnorm(x, g):
    x32 = x.astype(jnp.float32)
    y = x32 * lax.rsqrt(jnp.mean(x32 * x32, axis=-1, keepdims=True) + EPS)
    return (y * g.astype(jnp.float32)).astype(x.dtype)


def layernorm(x, g, b):
    x32 = x.astype(jnp.float32)
    mu = jnp.mean(x32, axis=-1, keepdims=True)
    xc = x32 - mu
    var = jnp.mean(xc * xc, axis=-1, keepdims=True)
    y = xc * lax.rsqrt(var + EPS) * g.astype(jnp.float32) + b.astype(jnp.float32)
    return y.astype(x.dtype)


def causal_dwconv(x, w):
    K, C = w.shape
    return lax.conv_general_dilated(
        x, w[:, None, :].astype(x.dtype), window_strides=(1,),
        padding=[(K - 1, 0)], dimension_numbers=("NWC", "WIO", "NWC"),
        feature_group_count=C)


def stick_breaking_attention(q, k, v, q_g, k_g):
    B_, L, _ = q.shape

    def heads(t):
        return t.reshape(B_, L, N_ATTN_HEADS, HEAD_DIM).transpose(0, 2, 1, 3)

    q = rmsnorm(heads(q), q_g)
    k = rmsnorm(heads(k), k_g)
    v = heads(v)
    pad = (-L) % BLOCK_Q
    padw = ((0, 0), (0, 0), (pad, 0), (0, 0))
    q = jnp.pad(q, padw)
    k = jnp.pad(k, padw)
    v = jnp.pad(v, padw)
    Lp = L + pad
    scale = HEAD_DIM ** -0.5
    outs = []
    for i in range(Lp // BLOCK_Q):
        start, end = i * BLOCK_Q, (i + 1) * BLOCK_Q
        qb = q[:, :, start:end].astype(jnp.float32)
        kb = k[:, :, :end].astype(jnp.float32)
        vb = v[:, :, :end]
        z = jnp.einsum("bhqd,bhkd->bhqk", qb, kb) * scale
        t_pos = jnp.arange(start, end)[:, None]
        s_pos = jnp.arange(end)[None, :]
        mask = (s_pos < t_pos) & (s_pos >= pad)
        log_beta = jax.nn.log_sigmoid(z)
        log_1m_beta = jnp.where(mask, log_beta - z, 0.0)
        tail = lax.cumsum(log_1m_beta, axis=3, reverse=True) - log_1m_beta
        a = jnp.where(mask, jnp.exp(log_beta + tail), 0.0)
        outs.append(jnp.einsum("bhqk,bhkd->bhqd", a.astype(vb.dtype), vb))
    o = jnp.concatenate(outs, axis=2)[:, :, pad:]
    return o.transpose(0, 2, 1, 3).reshape(B_, L, GROUP_W)


def multiscale_pool(x, pool_w, pool_scale):
    B_, L, C = x.shape
    x32 = x.astype(jnp.float32)
    cs = jnp.concatenate([jnp.zeros((B_, 1, C), jnp.float32), jnp.cumsum(x32, axis=1)], axis=1)
    t = jnp.arange(L)
    outs = []
    for gi, w in enumerate(POOL_WINDOWS):
        sl = slice(gi * POOL_GW, (gi + 1) * POOL_GW)
        csg = cs[:, :, sl]
        lo = jnp.maximum(t + 1 - w, 0)
        window_sum = csg[:, 1:] - csg[:, lo]
        count = jnp.minimum(t + 1, w).astype(jnp.float32)[None, :, None]
        pooled = (window_sum / count - x32[:, :, sl]).astype(x.dtype)
        outs.append(pooled @ pool_w[gi])
    return jnp.concatenate(outs, axis=-1) * pool_scale


def hybrid_layer(h, norm_g, w_in, conf_dw_w, conf_dw_b, conf_ln_g, conf_ln_b, conf_pw_w,
                 sc_conv_w, q_norm_g, k_norm_g, pool_w, pool_scale, w_out):
    u = rmsnorm(h, norm_g)
    p = u @ w_in
    (ca, cb, cg, sb, sc, sx, sg, q, k, v, ag, pi, pg) = jnp.split(p, N_CHUNKS, axis=-1)
    a = ca * jax.nn.sigmoid(cb)
    a = causal_dwconv(a, conf_dw_w) + conf_dw_b
    a = jax.nn.silu(layernorm(a, conf_ln_g, conf_ln_b)) @ conf_pw_w
    ya = a * jax.nn.silu(cg)
    yb = sb * causal_dwconv(sc * sx, sc_conv_w) * jax.nn.silu(sg)
    yc = stick_breaking_attention(q, k, v, q_norm_g, k_norm_g) * jax.nn.silu(ag)
    yd = multiscale_pool(pi, pool_w, pool_scale) * jax.nn.silu(pg)
    y = jnp.concatenate([ya, yb, yc, yd], axis=-1) @ w_out
    return h + y


def setup_inputs(seed: int = 0) -> dict:
    key = jax.random.key(seed)
    ks = jax.random.split(key, 16)
    f32 = jnp.float32
    nrm = lambda k, s: jax.random.normal(k, s, f32)
    return {
        "x": nrm(ks[0], (BATCH, SEQ, D_MODEL)),
        "meta_tokens": nrm(ks[1], (N_META, D_MODEL)),
        "norm_g": 1.0 + 0.02 * nrm(ks[2], (DEPTH, D_MODEL)),
        "w_in": nrm(ks[3], (DEPTH, D_MODEL, D_IN)) * D_MODEL ** -0.5,
        "conf_dw_w": nrm(ks[4], (DEPTH, CONF_K, GROUP_W)) * CONF_K ** -0.5,
        "conf_dw_b": 0.02 * nrm(ks[5], (DEPTH, GROUP_W)),
        "conf_ln_g": 1.0 + 0.02 * nrm(ks[6], (DEPTH, GROUP_W)),
        "conf_ln_b": 0.02 * nrm(ks[7], (DEPTH, GROUP_W)),
        "conf_pw_w": nrm(ks[8], (DEPTH, GROUP_W, GROUP_W)) * GROUP_W ** -0.5,
        "sc_conv_w": nrm(ks[9], (DEPTH, SC_K, GROUP_W)) * SC_K ** -0.5,
        "q_norm_g": 1.0 + 0.02 * nrm(ks[10], (DEPTH, HEAD_DIM)),
        "k_norm_g": 1.0 + 0.02 * nrm(ks[11], (DEPTH, HEAD_DIM)),
        "pool_w": nrm(ks[12], (DEPTH, len(POOL_WINDOWS), POOL_GW, POOL_GW)) * POOL_GW ** -0.5,
        "pool_scale": 1.0 + 0.02 * nrm(ks[13], (DEPTH, GROUP_W)),
        "w_out": nrm(ks[14], (DEPTH, D_MIX, D_MODEL)) * D_MIX ** -0.5,
    }


def reference(x, meta_tokens, norm_g, w_in, conf_dw_w, conf_dw_b, conf_ln_g, conf_ln_b,
              conf_pw_w, sc_conv_w, q_norm_g, k_norm_g, pool_w, pool_scale, w_out):
    B_ = x.shape[0]
    meta = jnp.broadcast_to(meta_tokens.astype(x.dtype)[None], (B_, N_META, D_MODEL))
    h = jnp.concatenate([meta, x], axis=1)
    for l in range(DEPTH):
        h = hybrid_layer(h, norm_g[l], w_in[l], conf_dw_w[l], conf_dw_b[l], conf_ln_g[l],
                         conf_ln_b[l], conf_pw_w[l], sc_conv_w[l], q_norm_g[l], k_norm_g[l],
                         pool_w[l], pool_scale[l], w_out[l])
    return h[:, N_META:]
```

```python
import functools

import jax
import jax.numpy as jnp
from jax import lax
from jax.experimental import pallas as pl
from jax.experimental.pallas import tpu as pltpu

D_MODEL = 4096
SEQ = 8192
DEPTH = 4
N_META = 16
GROUP_W = 1024
HEAD_DIM = 128
N_HEADS = GROUP_W // HEAD_DIM
CONF_K = 31
SC_K = 3
POOL_WINDOWS = (2, 4, 8, 16)
POOL_GW = GROUP_W // len(POOL_WINDOWS)
N_CHUNKS = 13
D_IN = N_CHUNKS * GROUP_W
EPS = 1e-6

V7X_LANES = 128
V7X_VMEM_BYTES = 64 * 1024 * 1024
VMEM_LIMIT_BYTES = V7X_VMEM_BYTES - 8 * 1024 * 1024

BLK = 256
L_REAL = N_META + SEQ
LP = -(-L_REAL // BLK) * BLK
PAD = LP - L_REAL
assert (PAD + N_META) % BLK == 0

ROW_TILE = 256
MM_TM = LP // 6
MM_TN = 512
MIX_TT = 128
CONV_RC = 32
HALO_A = 32
HALO_B = 8
HALO_D = 16
assert LP % MM_TM == 0 and MM_TM % 8 == 0 and LP % MIX_TT == 0 and PAD >= HALO_A

F32 = jnp.float32
BF16 = jnp.bfloat16


def _sigmoid(x):
    return 1.0 / (1.0 + jnp.exp(-x))


def _silu(x):
    return x * _sigmoid(x)


def _rmsnorm_kernel(h_ref, g_ref, u_ref):
    x = h_ref[...]
    ms = jnp.mean(x * x, axis=-1, keepdims=True)
    u_ref[...] = (x * lax.rsqrt(ms + EPS) * g_ref[...]).astype(BF16)


def _rmsnorm(hp, g):
    return pl.pallas_call(
        _rmsnorm_kernel,
        grid=(LP // ROW_TILE,),
        in_specs=[pl.BlockSpec((ROW_TILE, D_MODEL), lambda i: (i, 0)),
                  pl.BlockSpec((1, D_MODEL), lambda i: (0, 0))],
        out_specs=pl.BlockSpec((ROW_TILE, D_MODEL), lambda i: (i, 0)),
        out_shape=jax.ShapeDtypeStruct((LP, D_MODEL), BF16),
        compiler_params=pltpu.CompilerParams(dimension_semantics=("arbitrary",),
                                             vmem_limit_bytes=VMEM_LIMIT_BYTES),
        name="rmsnorm",
    )(hp, g.reshape(1, D_MODEL))


def _proj_in_kernel(u_ref, w_ref, p_ref):
    p_ref[...] = jnp.dot(u_ref[...], w_ref[...], preferred_element_type=F32)


def _proj_in(u, w_bf16):
    return pl.pallas_call(
        _proj_in_kernel,
        grid=(LP // MM_TM, D_IN // MM_TN),
        in_specs=[pl.BlockSpec((MM_TM, D_MODEL), lambda i, j: (i, 0)),
                  pl.BlockSpec((D_MODEL, MM_TN), lambda i, j: (0, j))],
        out_specs=pl.BlockSpec((MM_TM, MM_TN), lambda i, j: (i, j)),
        out_shape=jax.ShapeDtypeStruct((LP, D_IN), F32),
        compiler_params=pltpu.CompilerParams(dimension_semantics=("arbitrary", "arbitrary"),
                                             vmem_limit_bytes=VMEM_LIMIT_BYTES),
        name="proj_in",
    )(u, w_bf16)


def _head_rmsnorm(x, g):
    outs = []
    for hd in range(N_HEADS):
        xh = x[:, hd * HEAD_DIM:(hd + 1) * HEAD_DIM]
        ms = jnp.mean(xh * xh, axis=-1, keepdims=True)
        outs.append(xh * lax.rsqrt(ms + EPS) * g)
    return jnp.concatenate(outs, axis=1)


def _mixer_kernel(ca, cb, cg, sb, sc, sx, sg, q, k, v, pi, pg,
                  ca_h, cb_h, sc_h, sx_h, pi_h,
                  dw_w, dw_b, ln_g, ln_b, pw_w, scw, qg, kg, pool_w, pool_scale,
                  ya_o, yb_o, yd_o, q_o, k_o, v_o,
                  a_ext, s_ext, x_ext, act):
    tt = MIX_TT
    a_ext[0:HALO_A, :] = ca_h[...] * _sigmoid(cb_h[...])
    a_ext[HALO_A:HALO_A + tt, :] = ca[...] * _sigmoid(cb[...])
    for r0 in range(0, tt, CONV_RC):
        acc = jnp.broadcast_to(dw_b[...], (CONV_RC, GROUP_W))
        for kk in range(CONF_K):
            off = r0 + HALO_A - (CONF_K - 1) + kk
            acc = acc + dw_w[kk:kk + 1, :] * a_ext[off:off + CONV_RC, :]
        mu = jnp.mean(acc, axis=-1, keepdims=True)
        xc = acc - mu
        var = jnp.mean(xc * xc, axis=-1, keepdims=True)
        y = xc * lax.rsqrt(var + EPS) * ln_g[...] + ln_b[...]
        act[r0:r0 + CONV_RC, :] = _silu(y).astype(BF16)
    ya = jnp.dot(act[...], pw_w[...], preferred_element_type=F32) * _silu(cg[...])
    ya_o[...] = ya.astype(BF16)

    s_ext[0:HALO_B, :] = sc_h[...] * sx_h[...]
    s_ext[HALO_B:HALO_B + tt, :] = sc[...] * sx[...]
    conv = jnp.zeros((tt, GROUP_W), F32)
    for kk in range(SC_K):
        off = HALO_B - (SC_K - 1) + kk
        conv = conv + scw[kk:kk + 1, :] * s_ext[off:off + tt, :]
    yb_o[...] = (sb[...] * conv * _silu(sg[...])).astype(BF16)

    x_ext[0:HALO_D, :] = pi_h[...]
    x_ext[HALO_D:HALO_D + tt, :] = pi[...]
    row = pl.program_id(0) * tt + lax.broadcasted_iota(jnp.int32, (tt, 1), 0)
    n_seen = row - (PAD - 1)
    yd_parts = []
    for gi, w in enumerate(POOL_WINDOWS):
        c0 = gi * POOL_GW
        ws = x_ext[HALO_D:HALO_D + tt, c0:c0 + POOL_GW]
        for jj in range(1, w):
            ws = ws + x_ext[HALO_D - jj:HALO_D - jj + tt, c0:c0 + POOL_GW]
        count = jnp.clip(n_seen, 1, w).astype(F32)
        pooled = ws / count - x_ext[HALO_D:HALO_D + tt, c0:c0 + POOL_GW]
        yd_parts.append(jnp.dot(pooled.astype(BF16), pool_w[gi], preferred_element_type=F32))
    yd = jnp.concatenate(yd_parts, axis=1) * pool_scale[...] * _silu(pg[...])
    yd_o[...] = yd.astype(BF16)

    q_o[...] = (_head_rmsnorm(q[...], qg[...]) * (HEAD_DIM ** -0.5)).astype(BF16)
    k_o[...] = _head_rmsnorm(k[...], kg[...]).astype(BF16)
    v_o[...] = v[...].astype(BF16)


(_CA, _CB, _CG, _SB, _SC, _SX, _SG, _Q, _K, _V, _AG, _PI, _PG) = range(N_CHUNKS)


def _mixers(p, dw_w, dw_b, ln_g, ln_b, pw_w_bf16, scw, qg, kg, pool_w_bf16, pool_scale):
    tt = MIX_TT

    def chunk(c):
        return pl.BlockSpec((tt, GROUP_W), lambda i, c=c: (i, c))

    def halo(c, rows):
        per = tt // rows
        return pl.BlockSpec((rows, GROUP_W), lambda i, c=c, per=per: (jnp.maximum(i * per - 1, 0), c))

    def const(shape):
        nd = len(shape)
        return pl.BlockSpec(shape, lambda i, nd=nd: (0,) * nd)

    main = [_CA, _CB, _CG, _SB, _SC, _SX, _SG, _Q, _K, _V, _PI, _PG]
    in_specs = [chunk(c) for c in main]
    in_specs += [halo(_CA, HALO_A), halo(_CB, HALO_A), halo(_SC, HALO_B), halo(_SX, HALO_B),
                 halo(_PI, HALO_D)]
    in_specs += [const((CONF_K, GROUP_W)), const((1, GROUP_W)), const((1, GROUP_W)), const((1, GROUP_W)),
                 const((GROUP_W, GROUP_W)), const((SC_K, GROUP_W)), const((1, HEAD_DIM)),
                 const((1, HEAD_DIM)), const((len(POOL_WINDOWS), POOL_GW, POOL_GW)),
                 const((1, GROUP_W))]
    out_spec = pl.BlockSpec((tt, GROUP_W), lambda i: (i, 0))
    out_sds = jax.ShapeDtypeStruct((LP, GROUP_W), BF16)
    return pl.pallas_call(
        _mixer_kernel,
        grid=(LP // tt,),
        in_specs=in_specs,
        out_specs=[out_spec] * 6,
        out_shape=[out_sds] * 6,
        scratch_shapes=[pltpu.VMEM((HALO_A + tt, GROUP_W), F32),
                        pltpu.VMEM((HALO_B + tt, GROUP_W), F32),
                        pltpu.VMEM((HALO_D + tt, GROUP_W), F32),
                        pltpu.VMEM((tt, GROUP_W), BF16)],
        compiler_params=pltpu.CompilerParams(dimension_semantics=("arbitrary",),
                                             vmem_limit_bytes=VMEM_LIMIT_BYTES),
        name="mixers",
    )(*([p] * 17), dw_w, dw_b.reshape(1, GROUP_W), ln_g.reshape(1, GROUP_W), ln_b.reshape(1, GROUP_W),
      pw_w_bf16, scw, qg.reshape(1, HEAD_DIM), kg.reshape(1, HEAD_DIM), pool_w_bf16,
      pool_scale.reshape(1, GROUP_W))


def _suffix_sum_matrix():
    j = jnp.arange(BLK)[:, None]
    s = jnp.arange(BLK)[None, :]
    tri = (j >= s).astype(BF16)
    return jnp.concatenate([tri, jnp.ones((BLK, V7X_LANES), BF16)], axis=1)


def _attn_kernel(q_ref, k_ref, v_ref, ag_ref, u_ref, o_ref, acc_ref, c_ref):
    m = pl.program_id(1)
    acc_ref[...] = jnp.zeros_like(acc_ref)
    c_ref[...] = jnp.zeros_like(c_ref)
    q = q_ref[...]

    def block(j, masked):
        start = pl.multiple_of(j * BLK, BLK)
        kb = k_ref[pl.ds(start, BLK), :]
        vb = v_ref[pl.ds(start, BLK), :]
        z = lax.dot_general(q, kb, (((1,), (1,)), ((), ())), preferred_element_type=F32)
        sp = jnp.maximum(z, 0.0) + jnp.log(1.0 + jnp.exp(-jnp.abs(z)))
        if masked:
            row = m * BLK + lax.broadcasted_iota(jnp.int32, (BLK, BLK), 0)
            col = j * BLK + lax.broadcasted_iota(jnp.int32, (BLK, BLK), 1)
            valid = jnp.logical_and(col < row, col >= PAD)
            sp = jnp.where(valid, sp, 0.0)
        hi = sp.astype(BF16)
        lo = (sp - hi.astype(F32)).astype(BF16)
        u = u_ref[...]
        ssum = (jnp.dot(hi, u, preferred_element_type=F32)
                + jnp.dot(lo, u, preferred_element_type=F32))
        c = c_ref[...]
        stot = ssum[:, :BLK] + jnp.concatenate([c, c], axis=1)
        a = jnp.exp(z - stot)
        if masked:
            a = jnp.where(valid, a, 0.0)
        acc_ref[...] += jnp.dot(a.astype(BF16), vb, preferred_element_type=F32)
        c_ref[...] = c + ssum[:, BLK:]

    block(m, True)

    def body(jj, carry):
        block(m - jj, False)
        return carry

    lax.fori_loop(1, m, body, 0)

    @pl.when(m > 0)
    def _():
        block(0, True)

    o_ref[...] = (acc_ref[...] * _silu(ag_ref[...])).astype(BF16)


def _attention(qn, kn, vb, p):
    nq = LP // BLK
    return pl.pallas_call(
        _attn_kernel,
        grid=(N_HEADS, nq),
        in_specs=[pl.BlockSpec((BLK, HEAD_DIM), lambda h, m: (m, h)),
                  pl.BlockSpec((LP, HEAD_DIM), lambda h, m: (0, h)),
                  pl.BlockSpec((LP, HEAD_DIM), lambda h, m: (0, h)),
                  pl.BlockSpec((BLK, HEAD_DIM), lambda h, m: (m, _AG * N_HEADS + h)),
                  pl.BlockSpec((BLK, BLK + V7X_LANES), lambda h, m: (0, 0))],
        out_specs=pl.BlockSpec((BLK, HEAD_DIM), lambda h, m: (m, h)),
        out_shape=jax.ShapeDtypeStruct((LP, GROUP_W), BF16),
        scratch_shapes=[pltpu.VMEM((BLK, HEAD_DIM), F32), pltpu.VMEM((BLK, V7X_LANES), F32)],
        compiler_params=pltpu.CompilerParams(dimension_semantics=("arbitrary", "arbitrary"),
                                             vmem_limit_bytes=VMEM_LIMIT_BYTES),
        name="sb_attention",
    )(qn, kn, vb, p, _suffix_sum_matrix())


def _proj_out_kernel(ya_ref, yb_ref, yc_ref, yd_ref, w_ref, h_ref, o_ref):
    y = jnp.concatenate([ya_ref[...], yb_ref[...], yc_ref[...], yd_ref[...]], axis=1)
    o_ref[...] = h_ref[...] + jnp.dot(y, w_ref[...], preferred_element_type=F32)


def _proj_out(ya, yb, yc, yd, w_bf16, hp):
    yspec = pl.BlockSpec((MM_TM, GROUP_W), lambda i, j: (i, 0))
    return pl.pallas_call(
        _proj_out_kernel,
        grid=(LP // MM_TM, D_MODEL // MM_TN),
        in_specs=[yspec, yspec, yspec, yspec,
                  pl.BlockSpec((D_MODEL, MM_TN), lambda i, j: (0, j)),
                  pl.BlockSpec((MM_TM, MM_TN), lambda i, j: (i, j))],
        out_specs=pl.BlockSpec((MM_TM, MM_TN), lambda i, j: (i, j)),
        out_shape=jax.ShapeDtypeStruct((LP, D_MODEL), F32),
        compiler_params=pltpu.CompilerParams(dimension_semantics=("arbitrary", "arbitrary"),
                                             vmem_limit_bytes=VMEM_LIMIT_BYTES),
        name="proj_out",
    )(ya, yb, yc, yd, w_bf16, hp)


def kernel(x, meta_tokens, norm_g, w_in, conf_dw_w, conf_dw_b, conf_ln_g, conf_ln_b, conf_pw_w,
           sc_conv_w, q_norm_g, k_norm_g, pool_w, pool_scale, w_out):
    assert x.shape == (1, SEQ, D_MODEL)
    hp = jnp.concatenate([jnp.zeros((PAD, D_MODEL), F32), meta_tokens.astype(F32), x[0]], axis=0)
    for l in range(DEPTH):
        u = _rmsnorm(hp, norm_g[l])
        p = _proj_in(u, w_in[l].astype(BF16))
        ya, yb, yd, qn, kn, vb = _mixers(p, conf_dw_w[l], conf_dw_b[l], conf_ln_g[l], conf_ln_b[l],
                                         conf_pw_w[l].astype(BF16), sc_conv_w[l], q_norm_g[l],
                                         k_norm_g[l], pool_w[l].astype(BF16), pool_scale[l])
        yc = _attention(qn, kn, vb, p)
        hp = _proj_out(ya, yb, yc, yd, w_out[l].astype(BF16), hp)
    return hp[PAD + N_META:][None]
```

```python
import jax
import jax.numpy as jnp
from jax import lax
from jax.experimental import pallas as pl
from jax.experimental.pallas import tpu as pltpu

D_MODEL = 4096
SEQ = 8192
DEPTH = 4
N_META = 16
GROUP_W = 1024
HEAD_DIM = 128
N_HEADS = GROUP_W // HEAD_DIM
CONF_K = 31
SC_K = 3
POOL_WINDOWS = (2, 4, 8, 16)
POOL_GW = GROUP_W // len(POOL_WINDOWS)
N_CHUNKS = 13
D_IN = N_CHUNKS * GROUP_W
EPS = 1e-6

V7X_LANES = 128
V7X_VMEM_BYTES = 64 * 1024 * 1024
VMEM_LIMIT_BYTES = V7X_VMEM_BYTES - 8 * 1024 * 1024

BLK = 256
ATTN_NH = 8
ATTN_SKEW = 2
LOG2E = 1.4426950408889634
L_REAL = N_META + SEQ
LP = -(-L_REAL // BLK) * BLK
PAD = LP - L_REAL
assert (PAD + N_META) % BLK == 0

ROW_TILE = 256
MM_TM = LP // 6
MM_TN = 512
MIX_TT = 128
CONV_RC = 32
HALO_A = 32
HALO_B = 8
HALO_D = 16
assert LP % MM_TM == 0 and MM_TM % 8 == 0 and LP % MIX_TT == 0 and PAD >= HALO_A

F32 = jnp.float32
BF16 = jnp.bfloat16


def _sigmoid(x):
    return 1.0 / (1.0 + jnp.exp(-x))


def _silu(x):
    return x * _sigmoid(x)


def _rmsnorm_kernel(h_ref, g_ref, u_ref):
    x = h_ref[...]
    ms = jnp.mean(x * x, axis=-1, keepdims=True)
    u_ref[...] = (x * lax.rsqrt(ms + EPS) * g_ref[...]).astype(BF16)


def _rmsnorm(hp, g):
    return pl.pallas_call(
        _rmsnorm_kernel,
        grid=(LP // ROW_TILE,),
        in_specs=[pl.BlockSpec((ROW_TILE, D_MODEL), lambda i: (i, 0)),
                  pl.BlockSpec((1, D_MODEL), lambda i: (0, 0))],
        out_specs=pl.BlockSpec((ROW_TILE, D_MODEL), lambda i: (i, 0)),
        out_shape=jax.ShapeDtypeStruct((LP, D_MODEL), BF16),
        compiler_params=pltpu.CompilerParams(dimension_semantics=("arbitrary",),
                                             vmem_limit_bytes=VMEM_LIMIT_BYTES),
        name="rmsnorm",
    )(hp, g.reshape(1, D_MODEL))


def _proj_in_kernel(u_ref, w_ref, p_ref):
    p_ref[...] = jnp.dot(u_ref[...], w_ref[...].astype(BF16), preferred_element_type=F32)


def _proj_in(u, w_all, layer):
    return pl.pallas_call(
        _proj_in_kernel,
        grid=(LP // MM_TM, D_IN // MM_TN),
        in_specs=[pl.BlockSpec((MM_TM, D_MODEL), lambda i, j: (i, 0)),
                  pl.BlockSpec((None, D_MODEL, MM_TN), lambda i, j: (layer, 0, j))],
        out_specs=pl.BlockSpec((MM_TM, MM_TN), lambda i, j: (i, j)),
        out_shape=jax.ShapeDtypeStruct((LP, D_IN), F32),
        compiler_params=pltpu.CompilerParams(dimension_semantics=("arbitrary", "arbitrary"),
                                             vmem_limit_bytes=VMEM_LIMIT_BYTES),
        name="proj_in",
    )(u, w_all)


def _head_rmsnorm(x, g):
    outs = []
    for hd in range(N_HEADS):
        xh = x[:, hd * HEAD_DIM:(hd + 1) * HEAD_DIM]
        ms = jnp.mean(xh * xh, axis=-1, keepdims=True)
        outs.append(xh * lax.rsqrt(ms + EPS) * g)
    return jnp.concatenate(outs, axis=1)


def _mixer_kernel(ca, cb, cg, sb, sc, sx, sg, q, k, v, pi, pg,
                  ca_h, cb_h, sc_h, sx_h, pi_h,
                  dw_w, dw_b, ln_g, ln_b, pw_w, scw, qg, kg, pool_w, pool_scale,
                  ya_o, yb_o, yd_o, q_o, k_o, v_o,
                  a_ext, s_ext, x_ext, act, a_sh):
    tt = MIX_TT
    a_ext[0:HALO_A, :] = ca_h[...] * _sigmoid(cb_h[...])
    a_ext[HALO_A:HALO_A + tt, :] = ca[...] * _sigmoid(cb[...])
    n_sh = HALO_A + tt - 8
    for s in range(1, 8):
        a_sh[s - 1] = a_ext[8 - s:8 - s + n_sh, :]
    for r0 in range(0, tt, CONV_RC):
        acc = jnp.broadcast_to(dw_b[...], (CONV_RC, GROUP_W))
        for kk in range(CONF_K):
            back = CONF_K - 1 - kk
            s, base = back % 8, r0 + HALO_A - (back - back % 8)
            if s == 0:
                tap = a_ext[base:base + CONV_RC, :]
            else:
                tap = a_sh[s - 1, base - 8:base - 8 + CONV_RC, :]
            acc = acc + dw_w[kk:kk + 1, :] * tap
        mu = jnp.mean(acc, axis=-1, keepdims=True)
        xc = acc - mu
        var = jnp.mean(xc * xc, axis=-1, keepdims=True)
        y = xc * lax.rsqrt(var + EPS) * ln_g[...] + ln_b[...]
        act[r0:r0 + CONV_RC, :] = _silu(y).astype(BF16)
    ya = jnp.dot(act[...], pw_w[...], preferred_element_type=F32) * _silu(cg[...])
    ya_o[...] = ya.astype(BF16)

    s_ext[0:HALO_B, :] = sc_h[...] * sx_h[...]
    s_ext[HALO_B:HALO_B + tt, :] = sc[...] * sx[...]
    conv = jnp.zeros((tt, GROUP_W), F32)
    for kk in range(SC_K):
        off = HALO_B - (SC_K - 1) + kk
        conv = conv + scw[kk:kk + 1, :] * s_ext[off:off + tt, :]
    yb_o[...] = (sb[...] * conv * _silu(sg[...])).astype(BF16)

    x_ext[0:HALO_D, :] = pi_h[...]
    x_ext[HALO_D:HALO_D + tt, :] = pi[...]
    row = pl.program_id(0) * tt + lax.broadcasted_iota(jnp.int32, (tt, 1), 0)
    n_seen = row - (PAD - 1)
    yd_parts = []
    for gi, w in enumerate(POOL_WINDOWS):
        c0 = gi * POOL_GW
        ws = x_ext[HALO_D:HALO_D + tt, c0:c0 + POOL_GW]
        for jj in range(1, w):
            ws = ws + x_ext[HALO_D - jj:HALO_D - jj + tt, c0:c0 + POOL_GW]
        count = jnp.clip(n_seen, 1, w).astype(F32)
        pooled = ws / count - x_ext[HALO_D:HALO_D + tt, c0:c0 + POOL_GW]
        yd_parts.append(jnp.dot(pooled.astype(BF16), pool_w[gi], preferred_element_type=F32))
    yd = jnp.concatenate(yd_parts, axis=1) * pool_scale[...] * _silu(pg[...])
    yd_o[...] = yd.astype(BF16)

    q_o[...] = (_head_rmsnorm(q[...], qg[...]) * (HEAD_DIM ** -0.5 * LOG2E)).astype(BF16)
    k_o[...] = _head_rmsnorm(k[...], kg[...]).astype(BF16)
    v_o[...] = v[...].astype(BF16)


(_CA, _CB, _CG, _SB, _SC, _SX, _SG, _Q, _K, _V, _AG, _PI, _PG) = range(N_CHUNKS)


def _mixers(p, dw_w, dw_b, ln_g, ln_b, pw_w_bf16, scw, qg, kg, pool_w_bf16, pool_scale):
    tt = MIX_TT

    def chunk(c):
        return pl.BlockSpec((tt, GROUP_W), lambda i, c=c: (i, c))

    def halo(c, rows):
        per = tt // rows
        return pl.BlockSpec((rows, GROUP_W), lambda i, c=c, per=per: (jnp.maximum(i * per - 1, 0), c))

    def const(shape):
        nd = len(shape)
        return pl.BlockSpec(shape, lambda i, nd=nd: (0,) * nd)

    main = [_CA, _CB, _CG, _SB, _SC, _SX, _SG, _Q, _K, _V, _PI, _PG]
    in_specs = [chunk(c) for c in main]
    in_specs += [halo(_CA, HALO_A), halo(_CB, HALO_A), halo(_SC, HALO_B), halo(_SX, HALO_B),
                 halo(_PI, HALO_D)]
    in_specs += [const((CONF_K, GROUP_W)), const((1, GROUP_W)), const((1, GROUP_W)), const((1, GROUP_W)),
                 const((GROUP_W, GROUP_W)), const((SC_K, GROUP_W)), const((1, HEAD_DIM)),
                 const((1, HEAD_DIM)), const((len(POOL_WINDOWS), POOL_GW, POOL_GW)),
                 const((1, GROUP_W))]
    out_spec = pl.BlockSpec((tt, GROUP_W), lambda i: (i, 0))
    out_sds = jax.ShapeDtypeStruct((LP, GROUP_W), BF16)
    return pl.pallas_call(
        _mixer_kernel,
        grid=(LP // tt,),
        in_specs=in_specs,
        out_specs=[out_spec] * 6,
        out_shape=[out_sds] * 6,
        scratch_shapes=[pltpu.VMEM((HALO_A + tt, GROUP_W), F32),
                        pltpu.VMEM((HALO_B + tt, GROUP_W), F32),
                        pltpu.VMEM((HALO_D + tt, GROUP_W), F32),
                        pltpu.VMEM((tt, GROUP_W), BF16),
                        pltpu.VMEM((7, HALO_A + tt - 8, GROUP_W), F32)],
        compiler_params=pltpu.CompilerParams(dimension_semantics=("arbitrary",),
                                             vmem_limit_bytes=VMEM_LIMIT_BYTES),
        name="mixers",
    )(*([p] * 17), dw_w, dw_b.reshape(1, GROUP_W), ln_g.reshape(1, GROUP_W), ln_b.reshape(1, GROUP_W),
      pw_w_bf16, scw, qg.reshape(1, HEAD_DIM), kg.reshape(1, HEAD_DIM), pool_w_bf16,
      pool_scale.reshape(1, GROUP_W))


def _suffix_sum_matrix():
    j = jnp.arange(BLK)[:, None]
    s = jnp.arange(BLK)[None, :]
    return (j >= s).astype(BF16)


def _neg_abs(x):
    bits = lax.bitcast_convert_type(x, jnp.int32) | jnp.int32(-2 ** 31)
    return lax.bitcast_convert_type(bits, F32)


def _attn_kernel(q_ref, k_ref, v_ref, ag_ref, u_ref, o_ref, acc_ref, c_ref):
    m = pl.program_id(1)
    acc_ref[...] = jnp.zeros_like(acc_ref)
    c_ref[...] = jnp.zeros_like(c_ref)
    q = q_ref[...]

    def block(j, masked):
        start = pl.multiple_of(j * BLK, BLK)
        kb = k_ref[pl.ds(start, BLK), :]
        vb = v_ref[pl.ds(start, BLK), :]
        if masked:
            row = m * BLK + lax.broadcasted_iota(jnp.int32, (BLK, BLK), 0)
            col = j * BLK + lax.broadcasted_iota(jnp.int32, (BLK, BLK), 1)
            valid = jnp.logical_and(col < row, col >= PAD)
        u = u_ref[...]
        zs, sps, wts = {}, {}, {}

        def scores(hd):
            sl = slice(hd * HEAD_DIM, (hd + 1) * HEAD_DIM)
            z = lax.dot_general(q[:, sl], kb[:, sl], (((1,), (1,)), ((), ())),
                                preferred_element_type=F32)
            sp = jnp.maximum(z, 0.0) + jnp.log(1.0 + jnp.exp2(_neg_abs(z))) * LOG2E
            zs[hd], sps[hd] = z, (jnp.where(valid, sp, 0.0) if masked else sp)

        def weights(hd):
            ssum = jnp.dot(sps[hd].astype(BF16), u, preferred_element_type=F32)
            c = c_ref[hd]
            a = jnp.exp2(zs[hd] - (ssum + jnp.concatenate([c, c], axis=1)))
            wts[hd] = jnp.where(valid, a, 0.0) if masked else a
            c_ref[hd] = c + jnp.broadcast_to(ssum[:, 0:1], (BLK, V7X_LANES))

        def values(hd):
            sl = slice(hd * HEAD_DIM, (hd + 1) * HEAD_DIM)
            acc_ref[hd] += jnp.dot(wts[hd].astype(BF16), vb[:, sl], preferred_element_type=F32)

        for t in range(ATTN_NH + 2 * ATTN_SKEW):
            if 0 <= t - 2 * ATTN_SKEW < ATTN_NH:
                values(t - 2 * ATTN_SKEW)
            if 0 <= t - ATTN_SKEW < ATTN_NH:
                weights(t - ATTN_SKEW)
            if t < ATTN_NH:
                scores(t)

    block(m, True)

    def body(jj, carry):
        block(m - jj, False)
        return carry

    lax.fori_loop(1, m, body, 0)

    @pl.when(m > 0)
    def _():
        block(0, True)

    acc = jnp.concatenate([acc_ref[hd] for hd in range(ATTN_NH)], axis=1)
    o_ref[...] = (acc * _silu(ag_ref[...])).astype(BF16)


def _attention(qn, kn, vb, p):
    nq = LP // BLK
    w = ATTN_NH * HEAD_DIM
    ag_col0 = _AG * (GROUP_W // w)
    return pl.pallas_call(
        _attn_kernel,
        grid=(N_HEADS // ATTN_NH, nq),
        in_specs=[pl.BlockSpec((BLK, w), lambda g, m: (m, g)),
                  pl.BlockSpec((LP, w), lambda g, m: (0, g), pipeline_mode=pl.Buffered(1)),
                  pl.BlockSpec((LP, w), lambda g, m: (0, g), pipeline_mode=pl.Buffered(1)),
                  pl.BlockSpec((BLK, w), lambda g, m: (m, ag_col0 + g)),
                  pl.BlockSpec((BLK, BLK), lambda g, m: (0, 0))],
        out_specs=pl.BlockSpec((BLK, w), lambda g, m: (m, g)),
        out_shape=jax.ShapeDtypeStruct((LP, GROUP_W), BF16),
        scratch_shapes=[pltpu.VMEM((ATTN_NH, BLK, HEAD_DIM), F32),
                        pltpu.VMEM((ATTN_NH, BLK, V7X_LANES), F32)],
        compiler_params=pltpu.CompilerParams(dimension_semantics=("arbitrary", "arbitrary"),
                                             vmem_limit_bytes=VMEM_LIMIT_BYTES),
        name="sb_attention",
    )(qn, kn, vb, p, _suffix_sum_matrix())


def _proj_out_kernel(ya_ref, yb_ref, yc_ref, yd_ref, w_ref, h_ref, o_ref):
    y = jnp.concatenate([ya_ref[...], yb_ref[...], yc_ref[...], yd_ref[...]], axis=1)
    o_ref[...] = h_ref[...] + jnp.dot(y, w_ref[...], preferred_element_type=F32)


def _proj_out(ya, yb, yc, yd, w_all_bf16, layer, hp):
    yspec = pl.BlockSpec((MM_TM, GROUP_W), lambda i, j: (i, 0))
    return pl.pallas_call(
        _proj_out_kernel,
        grid=(LP // MM_TM, D_MODEL // MM_TN),
        in_specs=[yspec, yspec, yspec, yspec,
                  pl.BlockSpec((None, D_MODEL, MM_TN), lambda i, j: (layer, 0, j)),
                  pl.BlockSpec((MM_TM, MM_TN), lambda i, j: (i, j))],
        out_specs=pl.BlockSpec((MM_TM, MM_TN), lambda i, j: (i, j)),
        out_shape=jax.ShapeDtypeStruct((LP, D_MODEL), F32),
        compiler_params=pltpu.CompilerParams(dimension_semantics=("arbitrary", "arbitrary"),
                                             vmem_limit_bytes=VMEM_LIMIT_BYTES),
        name="proj_out",
    )(ya, yb, yc, yd, w_all_bf16, hp)


def kernel(x, meta_tokens, norm_g, w_in, conf_dw_w, conf_dw_b, conf_ln_g, conf_ln_b, conf_pw_w,
           sc_conv_w, q_norm_g, k_norm_g, pool_w, pool_scale, w_out):
    assert x.shape == (1, SEQ, D_MODEL)
    hp = jnp.concatenate([jnp.zeros((PAD, D_MODEL), F32), meta_tokens.astype(F32), x[0]], axis=0)
    w_out_bf16 = w_out.astype(BF16)
    for l in range(DEPTH):
        u = _rmsnorm(hp, norm_g[l])
        p = _proj_in(u, w_in, l)
        ya, yb, yd, qn, kn, vb = _mixers(p, conf_dw_w[l], conf_dw_b[l], conf_ln_g[l], conf_ln_b[l],
                                         conf_pw_w[l].astype(BF16), sc_conv_w[l], q_norm_g[l],
                                         k_norm_g[l], pool_w[l].astype(BF16), pool_scale[l])
        yc = _attention(qn, kn, vb, p)
        hp = _proj_out(ya, yb, yc, yd, w_out_bf16, l, hp)
    return hp[PAD + N_META:][None]
```

```python
import jax
import jax.numpy as jnp
from jax import lax
from jax.experimental import pallas as pl
from jax.experimental.pallas import tpu as pltpu

D_MODEL = 4096
SEQ = 8192
DEPTH = 4
N_META = 16
GROUP_W = 1024
HEAD_DIM = 128
N_HEADS = GROUP_W // HEAD_DIM
CONF_K = 31
SC_K = 3
POOL_WINDOWS = (2, 4, 8, 16)
POOL_GW = GROUP_W // len(POOL_WINDOWS)
N_CHUNKS = 13
D_IN = N_CHUNKS * GROUP_W
EPS = 1e-6

V7X_LANES = 128
V7X_VMEM_BYTES = 64 * 1024 * 1024
VMEM_LIMIT_BYTES = V7X_VMEM_BYTES - 8 * 1024 * 1024

BLK = 256
ATTN_NH = 8
ATTN_SKEW = 2
LOG2E = 1.4426950408889634
MASKED_SCORE = -1e30
EXP2_ZERO_BELOW = 152.0
L_REAL = N_META + SEQ
LP = -(-L_REAL // BLK) * BLK
PAD = LP - L_REAL
assert (PAD + N_META) % BLK == 0

ROW_TILE = 256
MM_TM = LP // 6
MM_TN = 512
MIX_TT = 128
CONV_RC = 32
HALO_A = 32
HALO_B = 8
HALO_D = 16
assert LP % MM_TM == 0 and MM_TM % 8 == 0 and LP % MIX_TT == 0 and PAD >= HALO_A

F32 = jnp.float32
BF16 = jnp.bfloat16


def _sigmoid(x):
    return 1.0 / (1.0 + jnp.exp(-x))


def _silu(x):
    return x * _sigmoid(x)


def _embed_kernel(x_ref, meta_ref, g_ref, h_ref, hg_ref, rs_ref):
    first = jnp.concatenate([jnp.zeros((PAD, D_MODEL), F32), meta_ref[...]], axis=0)
    h = jnp.where(pl.program_id(0) == 0, first, x_ref[...])
    h_ref[...] = h
    hg_ref[...] = (h * g_ref[...]).astype(BF16)
    rs_ref[...] = lax.rsqrt(jnp.mean(h * h, axis=-1, keepdims=True) + EPS)


def _embed(x2d, meta, g):
    assert PAD + N_META == ROW_TILE
    row = lambda i: (i, 0)
    return pl.pallas_call(
        _embed_kernel,
        grid=(LP // ROW_TILE,),
        in_specs=[pl.BlockSpec((ROW_TILE, D_MODEL), lambda i: (jnp.maximum(i - 1, 0), 0)),
                  pl.BlockSpec((N_META, D_MODEL), lambda i: (0, 0)),
                  pl.BlockSpec((1, D_MODEL), lambda i: (0, 0))],
        out_specs=[pl.BlockSpec((ROW_TILE, D_MODEL), row), pl.BlockSpec((ROW_TILE, D_MODEL), row),
                   pl.BlockSpec((ROW_TILE, 1), row)],
        out_shape=[jax.ShapeDtypeStruct((LP, D_MODEL), F32), jax.ShapeDtypeStruct((LP, D_MODEL), BF16),
                   jax.ShapeDtypeStruct((LP, 1), F32)],
        compiler_params=pltpu.CompilerParams(dimension_semantics=("arbitrary",),
                                             vmem_limit_bytes=VMEM_LIMIT_BYTES),
        name="embed",
    )(x2d, meta, g.reshape(1, D_MODEL))


def _proj_in_kernel(hg_ref, rs_ref, w_ref, p_ref):
    acc = jnp.dot(hg_ref[...], w_ref[...].astype(BF16), preferred_element_type=F32)
    p_ref[...] = acc * rs_ref[...]


def _proj_in(hg, rs, w_all, layer):
    return pl.pallas_call(
        _proj_in_kernel,
        grid=(LP // MM_TM, D_IN // MM_TN),
        in_specs=[pl.BlockSpec((MM_TM, D_MODEL), lambda i, j: (i, 0)),
                  pl.BlockSpec((MM_TM, 1), lambda i, j: (i, 0)),
                  pl.BlockSpec((None, D_MODEL, MM_TN), lambda i, j: (layer, 0, j))],
        out_specs=pl.BlockSpec((MM_TM, MM_TN), lambda i, j: (i, j)),
        out_shape=jax.ShapeDtypeStruct((LP, D_IN), F32),
        compiler_params=pltpu.CompilerParams(dimension_semantics=("arbitrary", "arbitrary"),
                                             vmem_limit_bytes=VMEM_LIMIT_BYTES),
        name="proj_in",
    )(hg, rs, w_all)


def _head_rmsnorm(x, g):
    outs = []
    for hd in range(N_HEADS):
        xh = x[:, hd * HEAD_DIM:(hd + 1) * HEAD_DIM]
        ms = jnp.mean(xh * xh, axis=-1, keepdims=True)
        outs.append(xh * lax.rsqrt(ms + EPS) * g)
    return jnp.concatenate(outs, axis=1)


def _mixer_kernel(ca, cb, cg, sb, sc, sx, sg, q, k, v, pi, pg,
                  ca_h, cb_h, sc_h, sx_h, pi_h,
                  dw_w, dw_b, ln_g, ln_b, pw_w, scw, qg, kg, pool_w, pool_scale,
                  ya_o, yb_o, yd_o, q_o, k_o, v_o,
                  a_ext, s_ext, x_ext, act, a_sh):
    tt = MIX_TT
    a_ext[0:HALO_A, :] = ca_h[...] * _sigmoid(cb_h[...])
    a_ext[HALO_A:HALO_A + tt, :] = ca[...] * _sigmoid(cb[...])
    n_sh = HALO_A + tt - 8
    for s in range(1, 8):
        a_sh[s - 1] = a_ext[8 - s:8 - s + n_sh, :]
    for r0 in range(0, tt, CONV_RC):
        acc = jnp.broadcast_to(dw_b[...], (CONV_RC, GROUP_W))
        for kk in range(CONF_K):
            back = CONF_K - 1 - kk
            s, base = back % 8, r0 + HALO_A - (back - back % 8)
            if s == 0:
                tap = a_ext[base:base + CONV_RC, :]
            else:
                tap = a_sh[s - 1, base - 8:base - 8 + CONV_RC, :]
            acc = acc + dw_w[kk:kk + 1, :] * tap
        mu = jnp.mean(acc, axis=-1, keepdims=True)
        xc = acc - mu
        var = jnp.mean(xc * xc, axis=-1, keepdims=True)
        y = xc * lax.rsqrt(var + EPS) * ln_g[...] + ln_b[...]
        act[r0:r0 + CONV_RC, :] = _silu(y).astype(BF16)
    ya = jnp.dot(act[...], pw_w[...], preferred_element_type=F32) * _silu(cg[...])
    ya_o[...] = ya.astype(BF16)

    s_ext[0:HALO_B, :] = sc_h[...] * sx_h[...]
    s_ext[HALO_B:HALO_B + tt, :] = sc[...] * sx[...]
    conv = jnp.zeros((tt, GROUP_W), F32)
    for kk in range(SC_K):
        off = HALO_B - (SC_K - 1) + kk
        conv = conv + scw[kk:kk + 1, :] * s_ext[off:off + tt, :]
    yb_o[...] = (sb[...] * conv * _silu(sg[...])).astype(BF16)

    x_ext[0:HALO_D, :] = pi_h[...]
    x_ext[HALO_D:HALO_D + tt, :] = pi[...]
    row = pl.program_id(0) * tt + lax.broadcasted_iota(jnp.int32, (tt, 1), 0)
    n_seen = row - (PAD - 1)
    yd_parts = []
    for gi, w in enumerate(POOL_WINDOWS):
        c0 = gi * POOL_GW
        ws = x_ext[HALO_D:HALO_D + tt, c0:c0 + POOL_GW]
        for jj in range(1, w):
            ws = ws + x_ext[HALO_D - jj:HALO_D - jj + tt, c0:c0 + POOL_GW]
        count = jnp.clip(n_seen, 1, w).astype(F32)
        pooled = ws / count - x_ext[HALO_D:HALO_D + tt, c0:c0 + POOL_GW]
        yd_parts.append(jnp.dot(pooled.astype(BF16), pool_w[gi], preferred_element_type=F32))
    yd = jnp.concatenate(yd_parts, axis=1) * pool_scale[...] * _silu(pg[...])
    yd_o[...] = yd.astype(BF16)

    q_o[...] = (_head_rmsnorm(q[...], qg[...]) * (HEAD_DIM ** -0.5 * LOG2E)).astype(BF16)
    k_o[...] = _head_rmsnorm(k[...], kg[...]).astype(BF16)
    v_o[...] = v[...].astype(BF16)


(_CA, _CB, _CG, _SB, _SC, _SX, _SG, _Q, _K, _V, _AG, _PI, _PG) = range(N_CHUNKS)


def _mixers(p, dw_w, dw_b, ln_g, ln_b, pw_w_bf16, scw, qg, kg, pool_w_bf16, pool_scale):
    tt = MIX_TT

    def chunk(c):
        return pl.BlockSpec((tt, GROUP_W), lambda i, c=c: (i, c))

    def halo(c, rows):
        per = tt // rows
        return pl.BlockSpec((rows, GROUP_W), lambda i, c=c, per=per: (jnp.maximum(i * per - 1, 0), c))

    def const(shape):
        nd = len(shape)
        return pl.BlockSpec(shape, lambda i, nd=nd: (0,) * nd)

    main = [_CA, _CB, _CG, _SB, _SC, _SX, _SG, _Q, _K, _V, _PI, _PG]
    in_specs = [chunk(c) for c in main]
    in_specs += [halo(_CA, HALO_A), halo(_CB, HALO_A), halo(_SC, HALO_B), halo(_SX, HALO_B),
                 halo(_PI, HALO_D)]
    in_specs += [const((CONF_K, GROUP_W)), const((1, GROUP_W)), const((1, GROUP_W)), const((1, GROUP_W)),
                 const((GROUP_W, GROUP_W)), const((SC_K, GROUP_W)), const((1, HEAD_DIM)),
                 const((1, HEAD_DIM)), const((len(POOL_WINDOWS), POOL_GW, POOL_GW)),
                 const((1, GROUP_W))]
    out_spec = pl.BlockSpec((tt, GROUP_W), lambda i: (i, 0))
    out_sds = jax.ShapeDtypeStruct((LP, GROUP_W), BF16)
    return pl.pallas_call(
        _mixer_kernel,
        grid=(LP // tt,),
        in_specs=in_specs,
        out_specs=[out_spec] * 6,
        out_shape=[out_sds] * 6,
        scratch_shapes=[pltpu.VMEM((HALO_A + tt, GROUP_W), F32),
                        pltpu.VMEM((HALO_B + tt, GROUP_W), F32),
                        pltpu.VMEM((HALO_D + tt, GROUP_W), F32),
                        pltpu.VMEM((tt, GROUP_W), BF16),
                        pltpu.VMEM((7, HALO_A + tt - 8, GROUP_W), F32)],
        compiler_params=pltpu.CompilerParams(dimension_semantics=("arbitrary",),
                                             vmem_limit_bytes=VMEM_LIMIT_BYTES),
        name="mixers",
    )(*([p] * 17), dw_w, dw_b.reshape(1, GROUP_W), ln_g.reshape(1, GROUP_W), ln_b.reshape(1, GROUP_W),
      pw_w_bf16, scw, qg.reshape(1, HEAD_DIM), kg.reshape(1, HEAD_DIM), pool_w_bf16,
      pool_scale.reshape(1, GROUP_W))


def _suffix_sum_matrix():
    j = jnp.arange(BLK)[:, None]
    s = jnp.arange(BLK)[None, :]
    return (j >= s).astype(BF16)


def _neg_abs(x):
    bits = lax.bitcast_convert_type(x, jnp.int32) | jnp.int32(-2 ** 31)
    return lax.bitcast_convert_type(bits, F32)


def _attn_kernel(thr_ref, q_ref, k_ref, v_ref, ag_ref, u_ref, o_ref, acc_ref, c_ref):
    m = pl.program_id(1)
    acc_ref[...] = jnp.zeros_like(acc_ref)
    c_ref[...] = jnp.zeros_like(c_ref)
    q = q_ref[...]

    def blocks(js, masked):
        kbs, vbs, valids = [], [], []
        for j in js:
            start = pl.multiple_of(j * BLK, BLK)
            kbs.append(k_ref[pl.ds(start, BLK), :])
            vbs.append(v_ref[pl.ds(start, BLK), :])
            if masked:
                row = m * BLK + lax.broadcasted_iota(jnp.int32, (BLK, BLK), 0)
                col = j * BLK + lax.broadcasted_iota(jnp.int32, (BLK, BLK), 1)
                valids.append(jnp.logical_and(col < row, col >= PAD))
        chains = [(b, hd) for b in range(len(js)) for hd in range(ATTN_NH)]
        u = u_ref[...]
        zs, sps, wts = {}, {}, {}

        def scores(ci):
            b, hd = chains[ci]
            sl = slice(hd * HEAD_DIM, (hd + 1) * HEAD_DIM)
            z = lax.dot_general(q[:, sl], kbs[b][:, sl], (((1,), (1,)), ((), ())),
                                preferred_element_type=F32)
            if masked:
                z = jnp.where(valids[b], z, MASKED_SCORE)
            zs[ci] = z
            sps[ci] = jnp.maximum(z, 0.0) + jnp.log(1.0 + jnp.exp2(_neg_abs(z))) * LOG2E

        def weights(ci):
            _, hd = chains[ci]
            ssum = jnp.dot(sps[ci].astype(BF16), u, preferred_element_type=F32)
            c = c_ref[hd]
            wts[ci] = jnp.exp2(zs[ci] - (ssum + jnp.concatenate([c, c], axis=1)))
            c_ref[hd] = c + jnp.broadcast_to(ssum[:, 0:1], (BLK, V7X_LANES))

        def values(ci):
            b, hd = chains[ci]
            sl = slice(hd * HEAD_DIM, (hd + 1) * HEAD_DIM)
            acc_ref[hd] += jnp.dot(wts[ci].astype(BF16), vbs[b][:, sl], preferred_element_type=F32)

        n = len(chains)
        for t in range(n + 2 * ATTN_SKEW):
            if 0 <= t - 2 * ATTN_SKEW < n:
                values(t - 2 * ATTN_SKEW)
            if 0 <= t - ATTN_SKEW < n:
                weights(t - ATTN_SKEW)
            if t < n:
                scores(t)

    score_bound = thr_ref[0]

    def more_to_do():
        return (jnp.min(c_ref[...]) < score_bound + EXP2_ZERO_BELOW).astype(jnp.int32)

    blocks([m], True)

    def cond(carry):
        jj, live = carry
        return jnp.logical_and(jj < m, live == 1)

    def body(carry):
        jj, _ = carry
        blocks([m - jj], False)
        return jj + 1, more_to_do()

    _, live = lax.while_loop(cond, body, (jnp.int32(1), more_to_do()))

    @pl.when(jnp.logical_and(m > 0, live == 1))
    def _():
        blocks([0], True)

    acc = jnp.concatenate([acc_ref[hd] for hd in range(ATTN_NH)], axis=1)
    o_ref[...] = (acc * _silu(ag_ref[...])).astype(BF16)


def _attention(qn, kn, vb, p, qg, kg):
    score_bound = (1.02 * LOG2E * HEAD_DIM ** 0.5) * jnp.max(jnp.abs(qg)) * jnp.max(jnp.abs(kg))
    score_bound = score_bound.astype(F32).reshape(1)
    nq = LP // BLK
    w = ATTN_NH * HEAD_DIM
    ag_col0 = _AG * (GROUP_W // w)
    return pl.pallas_call(
        _attn_kernel,
        grid=(N_HEADS // ATTN_NH, nq),
        in_specs=[pl.BlockSpec(memory_space=pltpu.SMEM),
                  pl.BlockSpec((BLK, w), lambda g, m: (m, g)),
                  pl.BlockSpec((LP, w), lambda g, m: (0, g), pipeline_mode=pl.Buffered(1)),
                  pl.BlockSpec((LP, w), lambda g, m: (0, g), pipeline_mode=pl.Buffered(1)),
                  pl.BlockSpec((BLK, w), lambda g, m: (m, ag_col0 + g)),
                  pl.BlockSpec((BLK, BLK), lambda g, m: (0, 0))],
        out_specs=pl.BlockSpec((BLK, w), lambda g, m: (m, g)),
        out_shape=jax.ShapeDtypeStruct((LP, GROUP_W), BF16),
        scratch_shapes=[pltpu.VMEM((ATTN_NH, BLK, HEAD_DIM), F32),
                        pltpu.VMEM((ATTN_NH, BLK, V7X_LANES), F32)],
        compiler_params=pltpu.CompilerParams(dimension_semantics=("arbitrary", "arbitrary"),
                                             vmem_limit_bytes=VMEM_LIMIT_BYTES),
        name="sb_attention",
    )(score_bound, qn, kn, vb, p, _suffix_sum_matrix())


def _proj_out_last_kernel(ya_ref, yb_ref, yc_ref, yd_ref, w_ref, h_ref, o_ref):
    y = jnp.concatenate([ya_ref[...], yb_ref[...], yc_ref[...], yd_ref[...]], axis=1)
    o_ref[...] = h_ref[...] + jnp.dot(y, w_ref[...], preferred_element_type=F32)


def _proj_out_kernel(ya_ref, yb_ref, yc_ref, yd_ref, w_ref, h_ref, g_ref, o_ref, hg_ref, rs_ref, ssq_ref):
    j = pl.program_id(1)
    y = jnp.concatenate([ya_ref[...], yb_ref[...], yc_ref[...], yd_ref[...]], axis=1)
    hn = h_ref[...] + jnp.dot(y, w_ref[...], preferred_element_type=F32)
    o_ref[...] = hn
    hg_ref[...] = (hn * g_ref[...]).astype(BF16)
    part = jnp.sum(hn * hn, axis=-1, keepdims=True)

    @pl.when(j == 0)
    def _():
        ssq_ref[...] = part

    @pl.when(j > 0)
    def _():
        ssq_ref[...] += part

    @pl.when(j == pl.num_programs(1) - 1)
    def _():
        rs_ref[...] = lax.rsqrt(ssq_ref[...] * (1.0 / D_MODEL) + EPS)


def _proj_out(ya, yb, yc, yd, w_all_bf16, layer, hp, g_next):
    yspec = pl.BlockSpec((MM_TM, GROUP_W), lambda i, j: (i, 0))
    tile = pl.BlockSpec((MM_TM, MM_TN), lambda i, j: (i, j))
    wspec = pl.BlockSpec((None, D_MODEL, MM_TN), lambda i, j: (layer, 0, j))
    params = pltpu.CompilerParams(dimension_semantics=("arbitrary", "arbitrary"),
                                  vmem_limit_bytes=VMEM_LIMIT_BYTES)
    if g_next is None:
        return pl.pallas_call(
            _proj_out_last_kernel,
            grid=(LP // MM_TM, D_MODEL // MM_TN),
            in_specs=[yspec, yspec, yspec, yspec, wspec, tile],
            out_specs=tile,
            out_shape=jax.ShapeDtypeStruct((LP, D_MODEL), F32),
            compiler_params=params,
            name="proj_out_last",
        )(ya, yb, yc, yd, w_all_bf16, hp)
    return pl.pallas_call(
        _proj_out_kernel,
        grid=(LP // MM_TM, D_MODEL // MM_TN),
        in_specs=[yspec, yspec, yspec, yspec, wspec, tile,
                  pl.BlockSpec((1, MM_TN), lambda i, j: (0, j))],
        out_specs=[tile, tile, pl.BlockSpec((MM_TM, 1), lambda i, j: (i, 0))],
        out_shape=[jax.ShapeDtypeStruct((LP, D_MODEL), F32), jax.ShapeDtypeStruct((LP, D_MODEL), BF16),
                   jax.ShapeDtypeStruct((LP, 1), F32)],
        scratch_shapes=[pltpu.VMEM((MM_TM, 1), F32)],
        compiler_params=params,
        name="proj_out",
    )(ya, yb, yc, yd, w_all_bf16, hp, g_next.reshape(1, D_MODEL))


def kernel(x, meta_tokens, norm_g, w_in, conf_dw_w, conf_dw_b, conf_ln_g, conf_ln_b, conf_pw_w,
           sc_conv_w, q_norm_g, k_norm_g, pool_w, pool_scale, w_out):
    assert x.shape == (1, SEQ, D_MODEL)
    hp, hg, rs = _embed(x.reshape(SEQ, D_MODEL), meta_tokens, norm_g[0])
    w_out_bf16 = w_out.astype(BF16)
    for l in range(DEPTH):
        p = _proj_in(hg, rs, w_in, l)
        ya, yb, yd, qn, kn, vb = _mixers(p, conf_dw_w[l], conf_dw_b[l], conf_ln_g[l], conf_ln_b[l],
                                         conf_pw_w[l].astype(BF16), sc_conv_w[l], q_norm_g[l],
                                         k_norm_g[l], pool_w[l].astype(BF16), pool_scale[l])
        yc = _attention(qn, kn, vb, p, q_norm_g[l], k_norm_g[l])
        if l + 1 < DEPTH:
            hp, hg, rs = _proj_out(ya, yb, yc, yd, w_out_bf16, l, hp, norm_g[l + 1])
        else:
            hp = _proj_out(ya, yb, yc, yd, w_out_bf16, l, hp, None)
    return hp[PAD + N_META:][None]
```

```python
import jax
import jax.numpy as jnp
from jax import lax
from jax.experimental import pallas as pl
from jax.experimental.pallas import tpu as pltpu

D_MODEL = 4096
SEQ = 8192
DEPTH = 4
N_META = 16
GROUP_W = 1024
HEAD_DIM = 128
N_HEADS = GROUP_W // HEAD_DIM
CONF_K = 31
SC_K = 3
POOL_WINDOWS = (2, 4, 8, 16)
POOL_GW = GROUP_W // len(POOL_WINDOWS)
N_CHUNKS = 13
D_IN = N_CHUNKS * GROUP_W
EPS = 1e-6

V7X_LANES = 128
V7X_VMEM_BYTES = 64 * 1024 * 1024
VMEM_LIMIT_BYTES = V7X_VMEM_BYTES - 8 * 1024 * 1024

BLK = 256
ATTN_NH = 8
ATTN_SKEW = 2
LOG2E = 1.4426950408889634
MASKED_SCORE = -1e30
EXP2_ZERO_BELOW = 152.0
L_REAL = N_META + SEQ
LP = -(-L_REAL // BLK) * BLK
PAD = LP - L_REAL
assert (PAD + N_META) % BLK == 0

ROW_TILE = 256
MM_TM = LP // 6
MM_TN = 512
LAST_TM = 1024
SEQ_ROW0 = LP - SEQ
MIX_TT = 128
CONV_RC = 32
HALO_A = 32
HALO_B = 8
HALO_D = 16
assert SEQ % LAST_TM == 0 and LP % MM_TM == 0 and MM_TM % 8 == 0 and LP % MIX_TT == 0 and PAD >= HALO_A

F32 = jnp.float32
BF16 = jnp.bfloat16


def _sigmoid(x):
    return 1.0 / (1.0 + jnp.exp(-x))


def _silu(x):
    return x * _sigmoid(x)


def _embed_kernel(x_ref, meta_ref, g_ref, h_ref, hg_ref, ssq_ref):
    first = jnp.concatenate([jnp.zeros((PAD, D_MODEL), F32), meta_ref[...]], axis=0)
    h = jnp.where(pl.program_id(0) == 0, first, x_ref[...])
    h_ref[...] = h
    hg_ref[...] = (h * g_ref[...]).astype(BF16)
    ssq_ref[...] = jnp.sum(h * h, axis=-1, keepdims=True)


def _embed(x2d, meta, g):
    assert PAD + N_META == ROW_TILE
    row = lambda i: (i, 0)
    return pl.pallas_call(
        _embed_kernel,
        grid=(LP // ROW_TILE,),
        in_specs=[pl.BlockSpec((ROW_TILE, D_MODEL), lambda i: (jnp.maximum(i - 1, 0), 0)),
                  pl.BlockSpec((N_META, D_MODEL), lambda i: (0, 0)),
                  pl.BlockSpec((1, D_MODEL), lambda i: (0, 0))],
        out_specs=[pl.BlockSpec((ROW_TILE, D_MODEL), row), pl.BlockSpec((ROW_TILE, D_MODEL), row),
                   pl.BlockSpec((ROW_TILE, 1), row)],
        out_shape=[jax.ShapeDtypeStruct((LP, D_MODEL), F32), jax.ShapeDtypeStruct((LP, D_MODEL), BF16),
                   jax.ShapeDtypeStruct((LP, 1), F32)],
        compiler_params=pltpu.CompilerParams(dimension_semantics=("arbitrary",),
                                             vmem_limit_bytes=VMEM_LIMIT_BYTES),
        name="embed",
    )(x2d, meta, g.reshape(1, D_MODEL))


def _proj_in_kernel(hg_ref, ssq_ref, w_ref, p_ref):
    acc = jnp.dot(hg_ref[...], w_ref[...].astype(BF16), preferred_element_type=F32)
    p_ref[...] = acc * lax.rsqrt(ssq_ref[...] * (1.0 / D_MODEL) + EPS)


def _proj_in(hg, ssq, w_all, layer):
    return pl.pallas_call(
        _proj_in_kernel,
        grid=(LP // MM_TM, D_IN // MM_TN),
        in_specs=[pl.BlockSpec((MM_TM, D_MODEL), lambda i, j: (i, 0)),
                  pl.BlockSpec((MM_TM, 1), lambda i, j: (i, 0)),
                  pl.BlockSpec((None, D_MODEL, MM_TN), lambda i, j: (layer, 0, j))],
        out_specs=pl.BlockSpec((MM_TM, MM_TN), lambda i, j: (i, j)),
        out_shape=jax.ShapeDtypeStruct((LP, D_IN), F32),
        compiler_params=pltpu.CompilerParams(dimension_semantics=("arbitrary", "arbitrary"),
                                             vmem_limit_bytes=VMEM_LIMIT_BYTES),
        name="proj_in",
    )(hg, ssq, w_all)


def _head_rmsnorm(x, g):
    outs = []
    for hd in range(N_HEADS):
        xh = x[:, hd * HEAD_DIM:(hd + 1) * HEAD_DIM]
        ms = jnp.mean(xh * xh, axis=-1, keepdims=True)
        outs.append(xh * lax.rsqrt(ms + EPS) * g)
    return jnp.concatenate(outs, axis=1)


def _mixer_kernel(ca, cb, cg, sb, sc, sx, sg, q, k, v, pi, pg,
                  ca_h, cb_h, sc_h, sx_h, pi_h,
                  dw_w, dw_b, ln_g, ln_b, pw_w, scw, qg, kg, pool_w, pool_scale,
                  ya_o, yb_o, yd_o, q_o, k_o, v_o,
                  a_ext, s_ext, x_ext, act, a_sh):
    tt = MIX_TT
    a_ext[0:HALO_A, :] = ca_h[...] * _sigmoid(cb_h[...])
    a_ext[HALO_A:HALO_A + tt, :] = ca[...] * _sigmoid(cb[...])
    n_sh = HALO_A + tt - 8
    for s in range(1, 8):
        a_sh[s - 1] = a_ext[8 - s:8 - s + n_sh, :]
    for r0 in range(0, tt, CONV_RC):
        acc = jnp.broadcast_to(dw_b[...], (CONV_RC, GROUP_W))
        for kk in range(CONF_K):
            back = CONF_K - 1 - kk
            s, base = back % 8, r0 + HALO_A - (back - back % 8)
            if s == 0:
                tap = a_ext[base:base + CONV_RC, :]
            else:
                tap = a_sh[s - 1, base - 8:base - 8 + CONV_RC, :]
            acc = acc + dw_w[kk:kk + 1, :] * tap
        mu = jnp.mean(acc, axis=-1, keepdims=True)
        xc = acc - mu
        var = jnp.mean(xc * xc, axis=-1, keepdims=True)
        y = xc * lax.rsqrt(var + EPS) * ln_g[...] + ln_b[...]
        act[r0:r0 + CONV_RC, :] = _silu(y).astype(BF16)
    ya = jnp.dot(act[...], pw_w[...], preferred_element_type=F32) * _silu(cg[...])
    ya_o[...] = ya.astype(BF16)

    s_ext[0:HALO_B, :] = sc_h[...] * sx_h[...]
    s_ext[HALO_B:HALO_B + tt, :] = sc[...] * sx[...]
    conv = jnp.zeros((tt, GROUP_W), F32)
    for kk in range(SC_K):
        off = HALO_B - (SC_K - 1) + kk
        conv = conv + scw[kk:kk + 1, :] * s_ext[off:off + tt, :]
    yb_o[...] = (sb[...] * conv * _silu(sg[...])).astype(BF16)

    x_ext[0:HALO_D, :] = pi_h[...]
    x_ext[HALO_D:HALO_D + tt, :] = pi[...]
    row = pl.program_id(0) * tt + lax.broadcasted_iota(jnp.int32, (tt, 1), 0)
    n_seen = row - (PAD - 1)
    yd_parts = []
    for gi, w in enumerate(POOL_WINDOWS):
        c0 = gi * POOL_GW
        ws = x_ext[HALO_D:HALO_D + tt, c0:c0 + POOL_GW]
        for jj in range(1, w):
            ws = ws + x_ext[HALO_D - jj:HALO_D - jj + tt, c0:c0 + POOL_GW]
        count = jnp.clip(n_seen, 1, w).astype(F32)
        pooled = ws / count - x_ext[HALO_D:HALO_D + tt, c0:c0 + POOL_GW]
        yd_parts.append(jnp.dot(pooled.astype(BF16), pool_w[gi], preferred_element_type=F32))
    yd = jnp.concatenate(yd_parts, axis=1) * pool_scale[...] * _silu(pg[...])
    yd_o[...] = yd.astype(BF16)

    q_o[...] = (_head_rmsnorm(q[...], qg[...]) * (HEAD_DIM ** -0.5 * LOG2E)).astype(BF16)
    k_o[...] = _head_rmsnorm(k[...], kg[...]).astype(BF16)
    v_o[...] = v[...].astype(BF16)


(_CA, _CB, _CG, _SB, _SC, _SX, _SG, _Q, _K, _V, _AG, _PI, _PG) = range(N_CHUNKS)


def _mixers(p, dw_w, dw_b, ln_g, ln_b, pw_w_bf16, scw, qg, kg, pool_w_bf16, pool_scale):
    tt = MIX_TT

    def chunk(c):
        return pl.BlockSpec((tt, GROUP_W), lambda i, c=c: (i, c))

    def halo(c, rows):
        per = tt // rows
        return pl.BlockSpec((rows, GROUP_W), lambda i, c=c, per=per: (jnp.maximum(i * per - 1, 0), c))

    def const(shape):
        nd = len(shape)
        return pl.BlockSpec(shape, lambda i, nd=nd: (0,) * nd)

    main = [_CA, _CB, _CG, _SB, _SC, _SX, _SG, _Q, _K, _V, _PI, _PG]
    in_specs = [chunk(c) for c in main]
    in_specs += [halo(_CA, HALO_A), halo(_CB, HALO_A), halo(_SC, HALO_B), halo(_SX, HALO_B),
                 halo(_PI, HALO_D)]
    in_specs += [const((CONF_K, GROUP_W)), const((1, GROUP_W)), const((1, GROUP_W)), const((1, GROUP_W)),
                 const((GROUP_W, GROUP_W)), const((SC_K, GROUP_W)), const((1, HEAD_DIM)),
                 const((1, HEAD_DIM)), const((len(POOL_WINDOWS), POOL_GW, POOL_GW)),
                 const((1, GROUP_W))]
    out_spec = pl.BlockSpec((tt, GROUP_W), lambda i: (i, 0))
    out_sds = jax.ShapeDtypeStruct((LP, GROUP_W), BF16)
    return pl.pallas_call(
        _mixer_kernel,
        grid=(LP // tt,),
        in_specs=in_specs,
        out_specs=[out_spec] * 6,
        out_shape=[out_sds] * 6,
        scratch_shapes=[pltpu.VMEM((HALO_A + tt, GROUP_W), F32),
                        pltpu.VMEM((HALO_B + tt, GROUP_W), F32),
                        pltpu.VMEM((HALO_D + tt, GROUP_W), F32),
                        pltpu.VMEM((tt, GROUP_W), BF16),
                        pltpu.VMEM((7, HALO_A + tt - 8, GROUP_W), F32)],
        compiler_params=pltpu.CompilerParams(dimension_semantics=("arbitrary",),
                                             vmem_limit_bytes=VMEM_LIMIT_BYTES),
        name="mixers",
    )(*([p] * 17), dw_w, dw_b.reshape(1, GROUP_W), ln_g.reshape(1, GROUP_W), ln_b.reshape(1, GROUP_W),
      pw_w_bf16, scw, qg.reshape(1, HEAD_DIM), kg.reshape(1, HEAD_DIM), pool_w_bf16,
      pool_scale.reshape(1, GROUP_W))


def _suffix_sum_matrix():
    j = jnp.arange(BLK)[:, None]
    s = jnp.arange(BLK)[None, :]
    return (j >= s).astype(BF16)


def _neg_abs(x):
    bits = lax.bitcast_convert_type(x, jnp.int32) | jnp.int32(-2 ** 31)
    return lax.bitcast_convert_type(bits, F32)


def _attn_kernel(thr_ref, q_ref, k_ref, v_ref, ag_ref, u_ref, o_ref, acc_ref, c_ref):
    m = pl.program_id(1)
    acc_ref[...] = jnp.zeros_like(acc_ref)
    c_ref[...] = jnp.zeros_like(c_ref)
    q = q_ref[...]

    def blocks(js, masked):
        kbs, vbs, valids = [], [], []
        for j in js:
            start = pl.multiple_of(j * BLK, BLK)
            kbs.append(k_ref[pl.ds(start, BLK), :])
            vbs.append(v_ref[pl.ds(start, BLK), :])
            if masked:
                row = m * BLK + lax.broadcasted_iota(jnp.int32, (BLK, BLK), 0)
                col = j * BLK + lax.broadcasted_iota(jnp.int32, (BLK, BLK), 1)
                valids.append(jnp.logical_and(col < row, col >= PAD))
        chains = [(b, hd) for b in range(len(js)) for hd in range(ATTN_NH)]
        u = u_ref[...]
        zs, sps, wts = {}, {}, {}

        def scores(ci):
            b, hd = chains[ci]
            sl = slice(hd * HEAD_DIM, (hd + 1) * HEAD_DIM)
            z = lax.dot_general(q[:, sl], kbs[b][:, sl], (((1,), (1,)), ((), ())),
                                preferred_element_type=F32)
            if masked:
                z = jnp.where(valids[b], z, MASKED_SCORE)
            zs[ci] = z
            sps[ci] = jnp.maximum(z, 0.0) + jnp.log(1.0 + jnp.exp2(_neg_abs(z))) * LOG2E

        def weights(ci):
            _, hd = chains[ci]
            ssum = jnp.dot(sps[ci].astype(BF16), u, preferred_element_type=F32)
            c = c_ref[hd]
            wts[ci] = jnp.exp2(zs[ci] - (ssum + jnp.concatenate([c, c], axis=1)))
            c_ref[hd] = c + jnp.broadcast_to(ssum[:, 0:1], (BLK, V7X_LANES))

        def values(ci):
            b, hd = chains[ci]
            sl = slice(hd * HEAD_DIM, (hd + 1) * HEAD_DIM)
            acc_ref[hd] += jnp.dot(wts[ci].astype(BF16), vbs[b][:, sl], preferred_element_type=F32)

        n = len(chains)
        for t in range(n + 2 * ATTN_SKEW):
            if 0 <= t - 2 * ATTN_SKEW < n:
                values(t - 2 * ATTN_SKEW)
            if 0 <= t - ATTN_SKEW < n:
                weights(t - ATTN_SKEW)
            if t < n:
                scores(t)

    score_bound = thr_ref[0]

    def more_to_do():
        return (jnp.min(c_ref[...]) < score_bound + EXP2_ZERO_BELOW).astype(jnp.int32)

    blocks([m], True)

    def cond(carry):
        jj, live = carry
        return jnp.logical_and(jj < m, live == 1)

    def body(carry):
        jj, _ = carry
        blocks([m - jj], False)
        return jj + 1, more_to_do()

    _, live = lax.while_loop(cond, body, (jnp.int32(1), more_to_do()))

    @pl.when(jnp.logical_and(m > 0, live == 1))
    def _():
        blocks([0], True)

    acc = jnp.concatenate([acc_ref[hd] for hd in range(ATTN_NH)], axis=1)
    o_ref[...] = (acc * _silu(ag_ref[...])).astype(BF16)


def _attention(qn, kn, vb, p, qg, kg):
    score_bound = (1.02 * LOG2E * HEAD_DIM ** 0.5) * jnp.max(jnp.abs(qg)) * jnp.max(jnp.abs(kg))
    score_bound = score_bound.astype(F32).reshape(1)
    nq = LP // BLK
    w = ATTN_NH * HEAD_DIM
    ag_col0 = _AG * (GROUP_W // w)
    return pl.pallas_call(
        _attn_kernel,
        grid=(N_HEADS // ATTN_NH, nq),
        in_specs=[pl.BlockSpec(memory_space=pltpu.SMEM),
                  pl.BlockSpec((BLK, w), lambda g, m: (m, g)),
                  pl.BlockSpec((LP, w), lambda g, m: (0, g), pipeline_mode=pl.Buffered(1)),
                  pl.BlockSpec((LP, w), lambda g, m: (0, g), pipeline_mode=pl.Buffered(1)),
                  pl.BlockSpec((BLK, w), lambda g, m: (m, ag_col0 + g)),
                  pl.BlockSpec((BLK, BLK), lambda g, m: (0, 0))],
        out_specs=pl.BlockSpec((BLK, w), lambda g, m: (m, g)),
        out_shape=jax.ShapeDtypeStruct((LP, GROUP_W), BF16),
        scratch_shapes=[pltpu.VMEM((ATTN_NH, BLK, HEAD_DIM), F32),
                        pltpu.VMEM((ATTN_NH, BLK, V7X_LANES), F32)],
        compiler_params=pltpu.CompilerParams(dimension_semantics=("arbitrary", "arbitrary"),
                                             vmem_limit_bytes=VMEM_LIMIT_BYTES),
        name="sb_attention",
    )(score_bound, qn, kn, vb, p, _suffix_sum_matrix())


def _proj_out_last_kernel(ya_ref, yb_ref, yc_ref, yd_ref, w_ref, h_ref, o_ref):
    y = jnp.concatenate([ya_ref[...], yb_ref[...], yc_ref[...], yd_ref[...]], axis=1)
    o_ref[...] = h_ref[...] + jnp.dot(y, w_ref[...], preferred_element_type=F32)


def _proj_out_kernel(ya_ref, yb_ref, yc_ref, yd_ref, w_ref, h_ref, g_ref, o_ref, hg_ref, ssq_ref):
    @pl.when(pl.program_id(1) == 0)
    def _():
        ssq_ref[...] = jnp.zeros_like(ssq_ref)

    y = jnp.concatenate([ya_ref[...], yb_ref[...], yc_ref[...], yd_ref[...]], axis=1)
    hn = h_ref[...] + jnp.dot(y, w_ref[...], preferred_element_type=F32)
    o_ref[...] = hn
    hg_ref[...] = (hn * g_ref[...]).astype(BF16)
    ssq_ref[...] += jnp.sum(hn * hn, axis=-1, keepdims=True)


def _proj_out(ya, yb, yc, yd, w_all_bf16, layer, hp, g_next):
    yspec = pl.BlockSpec((MM_TM, GROUP_W), lambda i, j: (i, 0))
    tile = pl.BlockSpec((MM_TM, MM_TN), lambda i, j: (i, j))
    wspec = pl.BlockSpec((None, D_MODEL, MM_TN), lambda i, j: (layer, 0, j))
    params = pltpu.CompilerParams(dimension_semantics=("arbitrary", "arbitrary"),
                                  vmem_limit_bytes=VMEM_LIMIT_BYTES)
    if g_next is None:
        def row_of(i):
            return pl.multiple_of(SEQ_ROW0 + i * LAST_TM, BLK)

        yrows = pl.BlockSpec((pl.Element(LAST_TM), pl.Element(GROUP_W)), lambda i, j: (row_of(i), 0))
        hrows = pl.BlockSpec((pl.Element(LAST_TM), pl.Element(MM_TN)),
                             lambda i, j: (row_of(i), pl.multiple_of(j * MM_TN, MM_TN)))
        return pl.pallas_call(
            _proj_out_last_kernel,
            grid=(SEQ // LAST_TM, D_MODEL // MM_TN),
            in_specs=[yrows, yrows, yrows, yrows, wspec, hrows],
            out_specs=pl.BlockSpec((LAST_TM, MM_TN), lambda i, j: (i, j)),
            out_shape=jax.ShapeDtypeStruct((SEQ, D_MODEL), F32),
            compiler_params=params,
            name="proj_out_last",
        )(ya, yb, yc, yd, w_all_bf16, hp)
    return pl.pallas_call(
        _proj_out_kernel,
        grid=(LP // MM_TM, D_MODEL // MM_TN),
        in_specs=[yspec, yspec, yspec, yspec, wspec, tile,
                  pl.BlockSpec((1, MM_TN), lambda i, j: (0, j))],
        out_specs=[tile, tile, pl.BlockSpec((MM_TM, 1), lambda i, j: (i, 0))],
        out_shape=[jax.ShapeDtypeStruct((LP, D_MODEL), F32), jax.ShapeDtypeStruct((LP, D_MODEL), BF16),
                   jax.ShapeDtypeStruct((LP, 1), F32)],
        compiler_params=params,
        name="proj_out",
    )(ya, yb, yc, yd, w_all_bf16, hp, g_next.reshape(1, D_MODEL))


def kernel(x, meta_tokens, norm_g, w_in, conf_dw_w, conf_dw_b, conf_ln_g, conf_ln_b, conf_pw_w,
           sc_conv_w, q_norm_g, k_norm_g, pool_w, pool_scale, w_out):
    assert x.shape == (1, SEQ, D_MODEL)
    hp, hg, ssq = _embed(x.reshape(SEQ, D_MODEL), meta_tokens, norm_g[0])
    w_out_bf16 = w_out.astype(BF16)
    for l in range(DEPTH):
        p = _proj_in(hg, ssq, w_in, l)
        ya, yb, yd, qn, kn, vb = _mixers(p, conf_dw_w[l], conf_dw_b[l], conf_ln_g[l], conf_ln_b[l],
                                         conf_pw_w[l].astype(BF16), sc_conv_w[l], q_norm_g[l],
                                         k_norm_g[l], pool_w[l].astype(BF16), pool_scale[l])
        yc = _attention(qn, kn, vb, p, q_norm_g[l], k_norm_g[l])
        if l + 1 < DEPTH:
            hp, hg, ssq = _proj_out(ya, yb, yc, yd, w_out_bf16, l, hp, norm_g[l + 1])
        else:
            out = _proj_out(ya, yb, yc, yd, w_out_bf16, l, hp, None)
    return out[None]
```

```python
import jax
import jax.numpy as jnp
from jax import lax
from jax.experimental import pallas as pl
from jax.experimental.pallas import tpu as pltpu

D_MODEL = 4096
SEQ = 8192
DEPTH = 4
N_META = 16
GROUP_W = 1024
HEAD_DIM = 128
N_HEADS = GROUP_W // HEAD_DIM
CONF_K = 31
SC_K = 3
POOL_WINDOWS = (2, 4, 8, 16)
POOL_GW = GROUP_W // len(POOL_WINDOWS)
N_CHUNKS = 13
D_IN = N_CHUNKS * GROUP_W
EPS = 1e-6

V7X_LANES = 128
V7X_VMEM_BYTES = 64 * 1024 * 1024
VMEM_LIMIT_BYTES = V7X_VMEM_BYTES - 8 * 1024 * 1024

BLK = 256
ATTN_NH = 8
ATTN_SKEW = 2
LOG2E = 1.4426950408889634
MASKED_SCORE = -1e30
EXP2_ZERO_BELOW = 152.0
L_REAL = N_META + SEQ
LP = -(-L_REAL // BLK) * BLK
PAD = LP - L_REAL
assert (PAD + N_META) % BLK == 0

ROW_TILE = 256
MM_TM = LP // 6
MM_TN = 512
LAST_TM = 1024
SEQ_ROW0 = LP - SEQ
MIX_TT = 256
CONV_RC = 32
HALO_A = 32
HALO_B = 8
HALO_D = 32
assert SEQ % LAST_TM == 0 and LP % MM_TM == 0 and MM_TM % 8 == 0 and LP % MIX_TT == 0 and PAD >= HALO_A

F32 = jnp.float32
BF16 = jnp.bfloat16


def _sigmoid(x):
    return 1.0 / (1.0 + jnp.exp(-x))


def _silu(x):
    return x * _sigmoid(x)


def _embed_kernel(x_ref, meta_ref, g_ref, h_ref, hg_ref, ssq_ref):
    first = jnp.concatenate([jnp.zeros((PAD, D_MODEL), F32), meta_ref[...]], axis=0)
    h = jnp.where(pl.program_id(0) == 0, first, x_ref[...])
    h_ref[...] = h
    hg_ref[...] = (h * g_ref[...]).astype(BF16)
    ssq_ref[...] = jnp.sum(h * h, axis=-1, keepdims=True)


def _embed(x2d, meta, g):
    assert PAD + N_META == ROW_TILE
    row = lambda i: (i, 0)
    return pl.pallas_call(
        _embed_kernel,
        grid=(LP // ROW_TILE,),
        in_specs=[pl.BlockSpec((ROW_TILE, D_MODEL), lambda i: (jnp.maximum(i - 1, 0), 0)),
                  pl.BlockSpec((N_META, D_MODEL), lambda i: (0, 0)),
                  pl.BlockSpec((1, D_MODEL), lambda i: (0, 0))],
        out_specs=[pl.BlockSpec((ROW_TILE, D_MODEL), row), pl.BlockSpec((ROW_TILE, D_MODEL), row),
                   pl.BlockSpec((ROW_TILE, 1), row)],
        out_shape=[jax.ShapeDtypeStruct((LP, D_MODEL), F32), jax.ShapeDtypeStruct((LP, D_MODEL), BF16),
                   jax.ShapeDtypeStruct((LP, 1), F32)],
        compiler_params=pltpu.CompilerParams(dimension_semantics=("arbitrary",),
                                             vmem_limit_bytes=VMEM_LIMIT_BYTES),
        name="embed",
    )(x2d, meta, g.reshape(1, D_MODEL))


def _proj_in_kernel(hg_ref, ssq_ref, w_ref, p_ref):
    acc = jnp.dot(hg_ref[...], w_ref[...].astype(BF16), preferred_element_type=F32)
    p_ref[...] = acc * lax.rsqrt(ssq_ref[...] * (1.0 / D_MODEL) + EPS)


def _proj_in(hg, ssq, w_all, layer):
    return pl.pallas_call(
        _proj_in_kernel,
        grid=(LP // MM_TM, D_IN // MM_TN),
        in_specs=[pl.BlockSpec((MM_TM, D_MODEL), lambda i, j: (i, 0)),
                  pl.BlockSpec((MM_TM, 1), lambda i, j: (i, 0)),
                  pl.BlockSpec((None, D_MODEL, MM_TN), lambda i, j: (layer, 0, j))],
        out_specs=pl.BlockSpec((MM_TM, MM_TN), lambda i, j: (i, j)),
        out_shape=jax.ShapeDtypeStruct((LP, D_IN), F32),
        compiler_params=pltpu.CompilerParams(dimension_semantics=("arbitrary", "arbitrary"),
                                             vmem_limit_bytes=VMEM_LIMIT_BYTES),
        name="proj_in",
    )(hg, ssq, w_all)


def _head_rmsnorm(x, g):
    outs = []
    for hd in range(N_HEADS):
        xh = x[:, hd * HEAD_DIM:(hd + 1) * HEAD_DIM]
        ms = jnp.mean(xh * xh, axis=-1, keepdims=True)
        outs.append(xh * lax.rsqrt(ms + EPS) * g)
    return jnp.concatenate(outs, axis=1)


def _mixer_kernel(ca, cb, cg, sb, sc, sx, sg, q, k, v, pi, pg,
                  ca_h, cb_h, sc_h, sx_h, pi_h,
                  dw_w, dw_b, ln_g, ln_b, pw_w, scw, qg, kg, pool_w, pool_scale,
                  ya_o, yb_o, yd_o, q_o, k_o, v_o,
                  a_ext, s_ext, x_ext, act, a_sh, pool_a, pool_b):
    tt = MIX_TT
    a_ext[0:HALO_A, :] = ca_h[...] * _sigmoid(cb_h[...])
    a_ext[HALO_A:HALO_A + tt, :] = ca[...] * _sigmoid(cb[...])
    n_sh = HALO_A + tt - 8
    for s in range(1, 8):
        a_sh[s - 1] = a_ext[8 - s:8 - s + n_sh, :]
    for r0 in range(0, tt, CONV_RC):
        acc = jnp.broadcast_to(dw_b[...], (CONV_RC, GROUP_W))
        for kk in range(CONF_K):
            back = CONF_K - 1 - kk
            s, base = back % 8, r0 + HALO_A - (back - back % 8)
            if s == 0:
                tap = a_ext[base:base + CONV_RC, :]
            else:
                tap = a_sh[s - 1, base - 8:base - 8 + CONV_RC, :]
            acc = acc + dw_w[kk:kk + 1, :] * tap
        mu = jnp.mean(acc, axis=-1, keepdims=True)
        xc = acc - mu
        var = jnp.mean(xc * xc, axis=-1, keepdims=True)
        y = xc * lax.rsqrt(var + EPS) * ln_g[...] + ln_b[...]
        act[r0:r0 + CONV_RC, :] = _silu(y).astype(BF16)
    ya = jnp.dot(act[...], pw_w[...], preferred_element_type=F32) * _silu(cg[...])
    ya_o[...] = ya.astype(BF16)

    s_ext[0:HALO_B, :] = sc_h[...] * sx_h[...]
    s_ext[HALO_B:HALO_B + tt, :] = sc[...] * sx[...]
    conv = jnp.zeros((tt, GROUP_W), F32)
    for kk in range(SC_K):
        off = HALO_B - (SC_K - 1) + kk
        conv = conv + scw[kk:kk + 1, :] * s_ext[off:off + tt, :]
    yb_o[...] = (sb[...] * conv * _silu(sg[...])).astype(BF16)

    x_ext[0:HALO_D, :] = pi_h[...]
    x_ext[HALO_D:HALO_D + tt, :] = pi[...]
    assert POOL_WINDOWS == (2, 4, 8, 16) and HALO_D == 32
    n_rows, g = HALO_D + tt, POOL_GW
    pool_a[8:n_rows, :] = x_ext[8:n_rows, :] + x_ext[7:n_rows - 1, :]
    pool_b[16:n_rows, g:] = pool_a[16:n_rows, g:] + pool_a[14:n_rows - 2, g:]
    pool_a[24:n_rows, 2 * g:] = pool_b[24:n_rows, 2 * g:] + pool_b[20:n_rows - 4, 2 * g:]
    pool_b[32:n_rows, 3 * g:] = pool_a[32:n_rows, 3 * g:] + pool_a[24:n_rows - 8, 3 * g:]
    window_sums = (pool_a, pool_b, pool_a, pool_b)
    row = pl.program_id(0) * tt + lax.broadcasted_iota(jnp.int32, (tt, 1), 0)
    n_seen = row - (PAD - 1)
    yd_parts = []
    for gi, w in enumerate(POOL_WINDOWS):
        c0 = gi * POOL_GW
        ws = window_sums[gi][HALO_D:HALO_D + tt, c0:c0 + POOL_GW]
        count = jnp.clip(n_seen, 1, w).astype(F32)
        pooled = ws / count - x_ext[HALO_D:HALO_D + tt, c0:c0 + POOL_GW]
        yd_parts.append(jnp.dot(pooled.astype(BF16), pool_w[gi], preferred_element_type=F32))
    yd = jnp.concatenate(yd_parts, axis=1) * pool_scale[...] * _silu(pg[...])
    yd_o[...] = yd.astype(BF16)

    q_o[...] = (_head_rmsnorm(q[...], qg[...]) * (HEAD_DIM ** -0.5 * LOG2E)).astype(BF16)
    k_o[...] = _head_rmsnorm(k[...], kg[...]).astype(BF16)
    v_o[...] = v[...].astype(BF16)


(_CA, _CB, _CG, _SB, _SC, _SX, _SG, _Q, _K, _V, _AG, _PI, _PG) = range(N_CHUNKS)


def _mixers(p, dw_w, dw_b, ln_g, ln_b, pw_w_bf16, scw, qg, kg, pool_w_bf16, pool_scale):
    tt = MIX_TT

    def chunk(c):
        return pl.BlockSpec((tt, GROUP_W), lambda i, c=c: (i, c))

    def halo(c, rows):
        per = tt // rows
        return pl.BlockSpec((rows, GROUP_W), lambda i, c=c, per=per: (jnp.maximum(i * per - 1, 0), c))

    def const(shape):
        nd = len(shape)
        return pl.BlockSpec(shape, lambda i, nd=nd: (0,) * nd)

    main = [_CA, _CB, _CG, _SB, _SC, _SX, _SG, _Q, _K, _V, _PI, _PG]
    in_specs = [chunk(c) for c in main]
    in_specs += [halo(_CA, HALO_A), halo(_CB, HALO_A), halo(_SC, HALO_B), halo(_SX, HALO_B),
                 halo(_PI, HALO_D)]
    in_specs += [const((CONF_K, GROUP_W)), const((1, GROUP_W)), const((1, GROUP_W)), const((1, GROUP_W)),
                 const((GROUP_W, GROUP_W)), const((SC_K, GROUP_W)), const((1, HEAD_DIM)),
                 const((1, HEAD_DIM)), const((len(POOL_WINDOWS), POOL_GW, POOL_GW)),
                 const((1, GROUP_W))]
    out_spec = pl.BlockSpec((tt, GROUP_W), lambda i: (i, 0))
    out_sds = jax.ShapeDtypeStruct((LP, GROUP_W), BF16)
    return pl.pallas_call(
        _mixer_kernel,
        grid=(LP // tt,),
        in_specs=in_specs,
        out_specs=[out_spec] * 6,
        out_shape=[out_sds] * 6,
        scratch_shapes=[pltpu.VMEM((HALO_A + tt, GROUP_W), F32),
                        pltpu.VMEM((HALO_B + tt, GROUP_W), F32),
                        pltpu.VMEM((HALO_D + tt, GROUP_W), F32),
                        pltpu.VMEM((tt, GROUP_W), BF16),
                        pltpu.VMEM((7, HALO_A + tt - 8, GROUP_W), F32),
                        pltpu.VMEM((HALO_D + tt, GROUP_W), F32),
                        pltpu.VMEM((HALO_D + tt, GROUP_W), F32)],
        compiler_params=pltpu.CompilerParams(dimension_semantics=("arbitrary",),
                                             vmem_limit_bytes=VMEM_LIMIT_BYTES),
        name="mixers",
    )(*([p] * 17), dw_w, dw_b.reshape(1, GROUP_W), ln_g.reshape(1, GROUP_W), ln_b.reshape(1, GROUP_W),
      pw_w_bf16, scw, qg.reshape(1, HEAD_DIM), kg.reshape(1, HEAD_DIM), pool_w_bf16,
      pool_scale.reshape(1, GROUP_W))


def _suffix_sum_matrix():
    j = jnp.arange(BLK)[:, None]
    s = jnp.arange(BLK)[None, :]
    return (j >= s).astype(BF16)


def _neg_abs(x):
    bits = lax.bitcast_convert_type(x, jnp.int32) | jnp.int32(-2 ** 31)
    return lax.bitcast_convert_type(bits, F32)


def _attn_kernel(thr_ref, q_ref, k_ref, v_ref, ag_ref, u_ref, o_ref, acc_ref, c_ref):
    m = pl.program_id(1)
    acc_ref[...] = jnp.zeros_like(acc_ref)
    c_ref[...] = jnp.zeros_like(c_ref)
    q = q_ref[...]

    def blocks(js, masked):
        kbs, vbs, valids = [], [], []
        for j in js:
            start = pl.multiple_of(j * BLK, BLK)
            kbs.append(k_ref[pl.ds(start, BLK), :])
            vbs.append(v_ref[pl.ds(start, BLK), :])
            if masked:
                row = m * BLK + lax.broadcasted_iota(jnp.int32, (BLK, BLK), 0)
                col = j * BLK + lax.broadcasted_iota(jnp.int32, (BLK, BLK), 1)
                valids.append(jnp.logical_and(col < row, col >= PAD))
        chains = [(b, hd) for b in range(len(js)) for hd in range(ATTN_NH)]
        u = u_ref[...]
        zs, sps, wts = {}, {}, {}

        def scores(ci):
            b, hd = chains[ci]
            sl = slice(hd * HEAD_DIM, (hd + 1) * HEAD_DIM)
            z = lax.dot_general(q[:, sl], kbs[b][:, sl], (((1,), (1,)), ((), ())),
                                preferred_element_type=F32)
            if masked:
                z = jnp.where(valids[b], z, MASKED_SCORE)
            zs[ci] = z
            sps[ci] = jnp.maximum(z, 0.0) + jnp.log(1.0 + jnp.exp2(_neg_abs(z))) * LOG2E

        def weights(ci):
            _, hd = chains[ci]
            ssum = jnp.dot(sps[ci].astype(BF16), u, preferred_element_type=F32)
            c = c_ref[hd]
            wts[ci] = jnp.exp2(zs[ci] - (ssum + jnp.concatenate([c, c], axis=1)))
            c_ref[hd] = c + jnp.broadcast_to(ssum[:, 0:1], (BLK, V7X_LANES))

        def values(ci):
            b, hd = chains[ci]
            sl = slice(hd * HEAD_DIM, (hd + 1) * HEAD_DIM)
            acc_ref[hd] += jnp.dot(wts[ci].astype(BF16), vbs[b][:, sl], preferred_element_type=F32)

        n = len(chains)
        for t in range(n + 2 * ATTN_SKEW):
            if 0 <= t - 2 * ATTN_SKEW < n:
                values(t - 2 * ATTN_SKEW)
            if 0 <= t - ATTN_SKEW < n:
                weights(t - ATTN_SKEW)
            if t < n:
                scores(t)

    score_bound = thr_ref[0]

    def more_to_do():
        return (jnp.min(c_ref[...]) < score_bound + EXP2_ZERO_BELOW).astype(jnp.int32)

    blocks([m], True)

    def cond(carry):
        jj, live = carry
        return jnp.logical_and(jj < m, live == 1)

    def body(carry):
        jj, _ = carry
        blocks([m - jj], False)
        return jj + 1, more_to_do()

    _, live = lax.while_loop(cond, body, (jnp.int32(1), more_to_do()))

    @pl.when(jnp.logical_and(m > 0, live == 1))
    def _():
        blocks([0], True)

    acc = jnp.concatenate([acc_ref[hd] for hd in range(ATTN_NH)], axis=1)
    o_ref[...] = (acc * _silu(ag_ref[...])).astype(BF16)


def _attention(qn, kn, vb, p, qg, kg):
    score_bound = (1.02 * LOG2E * HEAD_DIM ** 0.5) * jnp.max(jnp.abs(qg)) * jnp.max(jnp.abs(kg))
    score_bound = score_bound.astype(F32).reshape(1)
    nq = LP // BLK
    w = ATTN_NH * HEAD_DIM
    ag_col0 = _AG * (GROUP_W // w)
    return pl.pallas_call(
        _attn_kernel,
        grid=(N_HEADS // ATTN_NH, nq),
        in_specs=[pl.BlockSpec(memory_space=pltpu.SMEM),
                  pl.BlockSpec((BLK, w), lambda g, m: (m, g)),
                  pl.BlockSpec((LP, w), lambda g, m: (0, g), pipeline_mode=pl.Buffered(1)),
                  pl.BlockSpec((LP, w), lambda g, m: (0, g), pipeline_mode=pl.Buffered(1)),
                  pl.BlockSpec((BLK, w), lambda g, m: (m, ag_col0 + g)),
                  pl.BlockSpec((BLK, BLK), lambda g, m: (0, 0))],
        out_specs=pl.BlockSpec((BLK, w), lambda g, m: (m, g)),
        out_shape=jax.ShapeDtypeStruct((LP, GROUP_W), BF16),
        scratch_shapes=[pltpu.VMEM((ATTN_NH, BLK, HEAD_DIM), F32),
                        pltpu.VMEM((ATTN_NH, BLK, V7X_LANES), F32)],
        compiler_params=pltpu.CompilerParams(dimension_semantics=("arbitrary", "arbitrary"),
                                             vmem_limit_bytes=VMEM_LIMIT_BYTES),
        name="sb_attention",
    )(score_bound, qn, kn, vb, p, _suffix_sum_matrix())


def _proj_out_last_kernel(ya_ref, yb_ref, yc_ref, yd_ref, w_ref, h_ref, o_ref):
    y = jnp.concatenate([ya_ref[...], yb_ref[...], yc_ref[...], yd_ref[...]], axis=1)
    o_ref[...] = h_ref[...] + jnp.dot(y, w_ref[...], preferred_element_type=F32)


def _proj_out_kernel(ya_ref, yb_ref, yc_ref, yd_ref, w_ref, h_ref, g_ref, o_ref, hg_ref, ssq_ref):
    @pl.when(pl.program_id(1) == 0)
    def _():
        ssq_ref[...] = jnp.zeros_like(ssq_ref)

    y = jnp.concatenate([ya_ref[...], yb_ref[...], yc_ref[...], yd_ref[...]], axis=1)
    hn = h_ref[...] + jnp.dot(y, w_ref[...], preferred_element_type=F32)
    o_ref[...] = hn
    hg_ref[...] = (hn * g_ref[...]).astype(BF16)
    ssq_ref[...] += jnp.sum(hn * hn, axis=-1, keepdims=True)


def _proj_out(ya, yb, yc, yd, w_all_bf16, layer, hp, g_next):
    yspec = pl.BlockSpec((MM_TM, GROUP_W), lambda i, j: (i, 0))
    tile = pl.BlockSpec((MM_TM, MM_TN), lambda i, j: (i, j))
    wspec = pl.BlockSpec((None, D_MODEL, MM_TN), lambda i, j: (layer, 0, j))
    params = pltpu.CompilerParams(dimension_semantics=("arbitrary", "arbitrary"),
                                  vmem_limit_bytes=VMEM_LIMIT_BYTES)
    if g_next is None:
        def row_of(i):
            return pl.multiple_of(SEQ_ROW0 + i * LAST_TM, BLK)

        yrows = pl.BlockSpec((pl.Element(LAST_TM), pl.Element(GROUP_W)), lambda i, j: (row_of(i), 0))
        hrows = pl.BlockSpec((pl.Element(LAST_TM), pl.Element(MM_TN)),
                             lambda i, j: (row_of(i), pl.multiple_of(j * MM_TN, MM_TN)))
        return pl.pallas_call(
            _proj_out_last_kernel,
            grid=(SEQ // LAST_TM, D_MODEL // MM_TN),
            in_specs=[yrows, yrows, yrows, yrows, wspec, hrows],
            out_specs=pl.BlockSpec((LAST_TM, MM_TN), lambda i, j: (i, j)),
            out_shape=jax.ShapeDtypeStruct((SEQ, D_MODEL), F32),
            compiler_params=params,
            name="proj_out_last",
        )(ya, yb, yc, yd, w_all_bf16, hp)
    return pl.pallas_call(
        _proj_out_kernel,
        grid=(LP // MM_TM, D_MODEL // MM_TN),
        in_specs=[yspec, yspec, yspec, yspec, wspec, tile,
                  pl.BlockSpec((1, MM_TN), lambda i, j: (0, j))],
        out_specs=[tile, tile, pl.BlockSpec((MM_TM, 1), lambda i, j: (i, 0))],
        out_shape=[jax.ShapeDtypeStruct((LP, D_MODEL), F32), jax.ShapeDtypeStruct((LP, D_MODEL), BF16),
                   jax.ShapeDtypeStruct((LP, 1), F32)],
        compiler_params=params,
        name="proj_out",
    )(ya, yb, yc, yd, w_all_bf16, hp, g_next.reshape(1, D_MODEL))


def kernel(x, meta_tokens, norm_g, w_in, conf_dw_w, conf_dw_b, conf_ln_g, conf_ln_b, conf_pw_w,
           sc_conv_w, q_norm_g, k_norm_g, pool_w, pool_scale, w_out):
    assert x.shape == (1, SEQ, D_MODEL)
    hp, hg, ssq = _embed(x.reshape(SEQ, D_MODEL), meta_tokens, norm_g[0])
    w_out_bf16 = w_out.astype(BF16)
    for l in range(DEPTH):
        p = _proj_in(hg, ssq, w_in, l)
        ya, yb, yd, qn, kn, vb = _mixers(p, conf_dw_w[l], conf_dw_b[l], conf_ln_g[l], conf_ln_b[l],
                                         conf_pw_w[l].astype(BF16), sc_conv_w[l], q_norm_g[l],
                                         k_norm_g[l], pool_w[l].astype(BF16), pool_scale[l])
        yc = _attention(qn, kn, vb, p, q_norm_g[l], k_norm_g[l])
        if l + 1 < DEPTH:
            hp, hg, ssq = _proj_out(ya, yb, yc, yd, w_out_bf16, l, hp, norm_g[l + 1])
        else:
            out = _proj_out(ya, yb, yc, yd, w_out_bf16, l, hp, None)
    return out[None]
```

```python
import jax
import jax.numpy as jnp
from jax import lax
from jax.experimental import pallas as pl
from jax.experimental.pallas import tpu as pltpu

D_MODEL = 4096
SEQ = 8192
DEPTH = 4
N_META = 16
GROUP_W = 1024
HEAD_DIM = 128
N_HEADS = GROUP_W // HEAD_DIM
CONF_K = 31
SC_K = 3
POOL_WINDOWS = (2, 4, 8, 16)
POOL_GW = GROUP_W // len(POOL_WINDOWS)
N_CHUNKS = 13
D_IN = N_CHUNKS * GROUP_W
EPS = 1e-6

V7X_LANES = 128
V7X_VMEM_BYTES = 64 * 1024 * 1024
VMEM_LIMIT_BYTES = V7X_VMEM_BYTES - 8 * 1024 * 1024

BLK = 256
ATTN_NH = 8
ATTN_SKEW = 2
LOG2E = 1.4426950408889634
MASKED_SCORE = -1e30
EXP2_ZERO_BELOW = 152.0
L_REAL = N_META + SEQ
LP = -(-L_REAL // BLK) * BLK
PAD = LP - L_REAL
assert (PAD + N_META) % BLK == 0

ROW_TILE = 256
MM_TM = LP // 6
MM_TN = 512
LAST_TM = 1024
SEQ_ROW0 = LP - SEQ
MIX_TT = 256
CONV_RC = 256
HALO_A = 32
HALO_B = 8
HALO_D = 32
assert SEQ % LAST_TM == 0 and LP % MM_TM == 0 and MM_TM % 8 == 0 and LP % MIX_TT == 0 and PAD >= HALO_A

F32 = jnp.float32
BF16 = jnp.bfloat16


def _sigmoid(x):
    return 1.0 / (1.0 + jnp.exp(-x))


def _silu(x):
    return x * _sigmoid(x)


def _embed_kernel(x_ref, meta_ref, g_ref, h_ref, hg_ref, ssq_ref):
    first = jnp.concatenate([jnp.zeros((PAD, D_MODEL), F32), meta_ref[...]], axis=0)
    h = jnp.where(pl.program_id(0) == 0, first, x_ref[...])
    h_ref[...] = h
    hg_ref[...] = (h * g_ref[...]).astype(BF16)
    ssq_ref[...] = jnp.sum(h * h, axis=-1, keepdims=True)


def _embed(x2d, meta, g):
    assert PAD + N_META == ROW_TILE
    row = lambda i: (i, 0)
    return pl.pallas_call(
        _embed_kernel,
        grid=(LP // ROW_TILE,),
        in_specs=[pl.BlockSpec((ROW_TILE, D_MODEL), lambda i: (jnp.maximum(i - 1, 0), 0)),
                  pl.BlockSpec((N_META, D_MODEL), lambda i: (0, 0)),
                  pl.BlockSpec((1, D_MODEL), lambda i: (0, 0))],
        out_specs=[pl.BlockSpec((ROW_TILE, D_MODEL), row), pl.BlockSpec((ROW_TILE, D_MODEL), row),
                   pl.BlockSpec((ROW_TILE, 1), row)],
        out_shape=[jax.ShapeDtypeStruct((LP, D_MODEL), F32), jax.ShapeDtypeStruct((LP, D_MODEL), BF16),
                   jax.ShapeDtypeStruct((LP, 1), F32)],
        compiler_params=pltpu.CompilerParams(dimension_semantics=("arbitrary",),
                                             vmem_limit_bytes=VMEM_LIMIT_BYTES),
        name="embed",
    )(x2d, meta, g.reshape(1, D_MODEL))


def _proj_in_kernel(hg_ref, ssq_ref, w_ref, p_ref):
    acc = jnp.dot(hg_ref[...], w_ref[...].astype(BF16), preferred_element_type=F32)
    p_ref[...] = acc * lax.rsqrt(ssq_ref[...] * (1.0 / D_MODEL) + EPS)


def _proj_in(hg, ssq, w_all, layer):
    return pl.pallas_call(
        _proj_in_kernel,
        grid=(LP // MM_TM, D_IN // MM_TN),
        in_specs=[pl.BlockSpec((MM_TM, D_MODEL), lambda i, j: (i, 0)),
                  pl.BlockSpec((MM_TM, 1), lambda i, j: (i, 0)),
                  pl.BlockSpec((None, D_MODEL, MM_TN), lambda i, j: (layer, 0, j))],
        out_specs=pl.BlockSpec((MM_TM, MM_TN), lambda i, j: (i, j)),
        out_shape=jax.ShapeDtypeStruct((LP, D_IN), F32),
        compiler_params=pltpu.CompilerParams(dimension_semantics=("arbitrary", "arbitrary"),
                                             vmem_limit_bytes=VMEM_LIMIT_BYTES),
        name="proj_in",
    )(hg, ssq, w_all)


def _head_rmsnorm(x, g):
    outs = []
    for hd in range(N_HEADS):
        xh = x[:, hd * HEAD_DIM:(hd + 1) * HEAD_DIM]
        ms = jnp.mean(xh * xh, axis=-1, keepdims=True)
        outs.append(xh * lax.rsqrt(ms + EPS) * g)
    return jnp.concatenate(outs, axis=1)


def _mixer_kernel(ca, cb, cg, sb, sc, sx, sg, q, k, v, pi, pg,
                  ca_h, cb_h, sc_h, sx_h, pi_h,
                  dw_w, dw_b, ln_g, ln_b, pw_w, scw, qg, kg, pool_w, pool_scale,
                  ya_o, yb_o, yd_o, q_o, k_o, v_o,
                  a_ext, s_ext, x_ext, act, a_sh, pool_a, pool_b):
    tt = MIX_TT
    a_ext[0:HALO_A, :] = ca_h[...] * _sigmoid(cb_h[...])
    a_ext[HALO_A:HALO_A + tt, :] = ca[...] * _sigmoid(cb[...])
    n_sh = HALO_A + tt - 8
    for s in range(1, 8):
        a_sh[s - 1] = a_ext[8 - s:8 - s + n_sh, :]
    for r0 in range(0, tt, CONV_RC):
        acc = jnp.broadcast_to(dw_b[...], (CONV_RC, GROUP_W))
        for kk in range(CONF_K):
            back = CONF_K - 1 - kk
            s, base = back % 8, r0 + HALO_A - (back - back % 8)
            if s == 0:
                tap = a_ext[base:base + CONV_RC, :]
            else:
                tap = a_sh[s - 1, base - 8:base - 8 + CONV_RC, :]
            acc = acc + dw_w[kk:kk + 1, :] * tap
        mu = jnp.mean(acc, axis=-1, keepdims=True)
        xc = acc - mu
        var = jnp.mean(xc * xc, axis=-1, keepdims=True)
        y = xc * lax.rsqrt(var + EPS) * ln_g[...] + ln_b[...]
        act[r0:r0 + CONV_RC, :] = _silu(y).astype(BF16)
    ya = jnp.dot(act[...], pw_w[...], preferred_element_type=F32) * _silu(cg[...])
    ya_o[...] = ya.astype(BF16)

    s_ext[0:HALO_B, :] = sc_h[...] * sx_h[...]
    s_ext[HALO_B:HALO_B + tt, :] = sc[...] * sx[...]
    conv = jnp.zeros((tt, GROUP_W), F32)
    for kk in range(SC_K):
        off = HALO_B - (SC_K - 1) + kk
        conv = conv + scw[kk:kk + 1, :] * s_ext[off:off + tt, :]
    yb_o[...] = (sb[...] * conv * _silu(sg[...])).astype(BF16)

    x_ext[0:HALO_D, :] = pi_h[...]
    x_ext[HALO_D:HALO_D + tt, :] = pi[...]
    assert POOL_WINDOWS == (2, 4, 8, 16) and HALO_D == 32
    n_rows, g = HALO_D + tt, POOL_GW
    pool_a[8:n_rows, :] = x_ext[8:n_rows, :] + x_ext[7:n_rows - 1, :]
    pool_b[16:n_rows, g:] = pool_a[16:n_rows, g:] + pool_a[14:n_rows - 2, g:]
    pool_a[24:n_rows, 2 * g:] = pool_b[24:n_rows, 2 * g:] + pool_b[20:n_rows - 4, 2 * g:]
    pool_b[32:n_rows, 3 * g:] = pool_a[32:n_rows, 3 * g:] + pool_a[24:n_rows - 8, 3 * g:]
    window_sums = (pool_a, pool_b, pool_a, pool_b)
    row = pl.program_id(0) * tt + lax.broadcasted_iota(jnp.int32, (tt, 1), 0)
    n_seen = row - (PAD - 1)
    yd_parts = []
    for gi, w in enumerate(POOL_WINDOWS):
        c0 = gi * POOL_GW
        ws = window_sums[gi][HALO_D:HALO_D + tt, c0:c0 + POOL_GW]
        count = jnp.clip(n_seen, 1, w).astype(F32)
        pooled = ws / count - x_ext[HALO_D:HALO_D + tt, c0:c0 + POOL_GW]
        yd_parts.append(jnp.dot(pooled.astype(BF16), pool_w[gi], preferred_element_type=F32))
    yd = jnp.concatenate(yd_parts, axis=1) * pool_scale[...] * _silu(pg[...])
    yd_o[...] = yd.astype(BF16)

    q_o[...] = (_head_rmsnorm(q[...], qg[...]) * (HEAD_DIM ** -0.5 * LOG2E)).astype(BF16)
    k_o[...] = _head_rmsnorm(k[...], kg[...]).astype(BF16)
    v_o[...] = v[...].astype(BF16)


(_CA, _CB, _CG, _SB, _SC, _SX, _SG, _Q, _K, _V, _AG, _PI, _PG) = range(N_CHUNKS)


def _mixers(p, dw_w, dw_b, ln_g, ln_b, pw_w_bf16, scw, qg, kg, pool_w_bf16, pool_scale):
    tt = MIX_TT

    def chunk(c):
        return pl.BlockSpec((tt, GROUP_W), lambda i, c=c: (i, c))

    def halo(c, rows):
        per = tt // rows
        return pl.BlockSpec((rows, GROUP_W), lambda i, c=c, per=per: (jnp.maximum(i * per - 1, 0), c))

    def const(shape):
        nd = len(shape)
        return pl.BlockSpec(shape, lambda i, nd=nd: (0,) * nd)

    main = [_CA, _CB, _CG, _SB, _SC, _SX, _SG, _Q, _K, _V, _PI, _PG]
    in_specs = [chunk(c) for c in main]
    in_specs += [halo(_CA, HALO_A), halo(_CB, HALO_A), halo(_SC, HALO_B), halo(_SX, HALO_B),
                 halo(_PI, HALO_D)]
    in_specs += [const((CONF_K, GROUP_W)), const((1, GROUP_W)), const((1, GROUP_W)), const((1, GROUP_W)),
                 const((GROUP_W, GROUP_W)), const((SC_K, GROUP_W)), const((1, HEAD_DIM)),
                 const((1, HEAD_DIM)), const((len(POOL_WINDOWS), POOL_GW, POOL_GW)),
                 const((1, GROUP_W))]
    out_spec = pl.BlockSpec((tt, GROUP_W), lambda i: (i, 0))
    out_sds = jax.ShapeDtypeStruct((LP, GROUP_W), BF16)
    return pl.pallas_call(
        _mixer_kernel,
        grid=(LP // tt,),
        in_specs=in_specs,
        out_specs=[out_spec] * 6,
        out_shape=[out_sds] * 6,
        scratch_shapes=[pltpu.VMEM((HALO_A + tt, GROUP_W), F32),
                        pltpu.VMEM((HALO_B + tt, GROUP_W), F32),
                        pltpu.VMEM((HALO_D + tt, GROUP_W), F32),
                        pltpu.VMEM((tt, GROUP_W), BF16),
                        pltpu.VMEM((7, HALO_A + tt - 8, GROUP_W), F32),
                        pltpu.VMEM((HALO_D + tt, GROUP_W), F32),
                        pltpu.VMEM((HALO_D + tt, GROUP_W), F32)],
        compiler_params=pltpu.CompilerParams(dimension_semantics=("arbitrary",),
                                             vmem_limit_bytes=VMEM_LIMIT_BYTES),
        name="mixers",
    )(*([p] * 17), dw_w, dw_b.reshape(1, GROUP_W), ln_g.reshape(1, GROUP_W), ln_b.reshape(1, GROUP_W),
      pw_w_bf16, scw, qg.reshape(1, HEAD_DIM), kg.reshape(1, HEAD_DIM), pool_w_bf16,
      pool_scale.reshape(1, GROUP_W))


def _suffix_sum_matrix():
    j = jnp.arange(BLK)[:, None]
    s = jnp.arange(BLK)[None, :]
    return (j >= s).astype(BF16)


def _neg_abs(x):
    bits = lax.bitcast_convert_type(x, jnp.int32) | jnp.int32(-2 ** 31)
    return lax.bitcast_convert_type(bits, F32)


def _attn_kernel(thr_ref, q_ref, k_ref, v_ref, ag_ref, u_ref, o_ref, acc_ref, c_ref):
    m = pl.program_id(1)
    acc_ref[...] = jnp.zeros_like(acc_ref)
    c_ref[...] = jnp.zeros_like(c_ref)
    q = q_ref[...]

    def blocks(js, masked):
        kbs, vbs, valids = [], [], []
        for j in js:
            start = pl.multiple_of(j * BLK, BLK)
            kbs.append(k_ref[pl.ds(start, BLK), :])
            vbs.append(v_ref[pl.ds(start, BLK), :])
            if masked:
                row = m * BLK + lax.broadcasted_iota(jnp.int32, (BLK, BLK), 0)
                col = j * BLK + lax.broadcasted_iota(jnp.int32, (BLK, BLK), 1)
                valids.append(jnp.logical_and(col < row, col >= PAD))
        chains = [(b, hd) for b in range(len(js)) for hd in range(ATTN_NH)]
        u = u_ref[...]
        zs, sps, wts = {}, {}, {}

        def scores(ci):
            b, hd = chains[ci]
            sl = slice(hd * HEAD_DIM, (hd + 1) * HEAD_DIM)
            z = lax.dot_general(q[:, sl], kbs[b][:, sl], (((1,), (1,)), ((), ())),
                                preferred_element_type=F32)
            if masked:
                z = jnp.where(valids[b], z, MASKED_SCORE)
            zs[ci] = z
            sps[ci] = jnp.maximum(z, 0.0) + jnp.log(1.0 + jnp.exp2(_neg_abs(z))) * LOG2E

        def weights(ci):
            _, hd = chains[ci]
            ssum = jnp.dot(sps[ci].astype(BF16), u, preferred_element_type=F32)
            c = c_ref[hd]
            wts[ci] = jnp.exp2(zs[ci] - (ssum + jnp.concatenate([c, c], axis=1)))
            c_ref[hd] = c + jnp.broadcast_to(ssum[:, 0:1], (BLK, V7X_LANES))

        def values(ci):
            b, hd = chains[ci]
            sl = slice(hd * HEAD_DIM, (hd + 1) * HEAD_DIM)
            acc_ref[hd] += jnp.dot(wts[ci].astype(BF16), vbs[b][:, sl], preferred_element_type=F32)

        n = len(chains)
        for t in range(n + 2 * ATTN_SKEW):
            if 0 <= t - 2 * ATTN_SKEW < n:
                values(t - 2 * ATTN_SKEW)
            if 0 <= t - ATTN_SKEW < n:
                weights(t - ATTN_SKEW)
            if t < n:
                scores(t)

    score_bound = thr_ref[0]

    def more_to_do():
        return (jnp.min(c_ref[...]) < score_bound + EXP2_ZERO_BELOW).astype(jnp.int32)

    blocks([m], True)

    def cond(carry):
        jj, live = carry
        return jnp.logical_and(jj < m, live == 1)

    def body(carry):
        jj, _ = carry
        blocks([m - jj], False)
        return jj + 1, more_to_do()

    _, live = lax.while_loop(cond, body, (jnp.int32(1), more_to_do()))

    @pl.when(jnp.logical_and(m > 0, live == 1))
    def _():
        blocks([0], True)

    acc = jnp.concatenate([acc_ref[hd] for hd in range(ATTN_NH)], axis=1)
    o_ref[...] = (acc * _silu(ag_ref[...])).astype(BF16)


def _attention(qn, kn, vb, p, qg, kg):
    score_bound = (1.02 * LOG2E * HEAD_DIM ** 0.5) * jnp.max(jnp.abs(qg)) * jnp.max(jnp.abs(kg))
    score_bound = score_bound.astype(F32).reshape(1)
    nq = LP // BLK
    w = ATTN_NH * HEAD_DIM
    ag_col0 = _AG * (GROUP_W // w)
    return pl.pallas_call(
        _attn_kernel,
        grid=(N_HEADS // ATTN_NH, nq),
        in_specs=[pl.BlockSpec(memory_space=pltpu.SMEM),
                  pl.BlockSpec((BLK, w), lambda g, m: (m, g)),
                  pl.BlockSpec((LP, w), lambda g, m: (0, g), pipeline_mode=pl.Buffered(1)),
                  pl.BlockSpec((LP, w), lambda g, m: (0, g), pipeline_mode=pl.Buffered(1)),
                  pl.BlockSpec((BLK, w), lambda g, m: (m, ag_col0 + g)),
                  pl.BlockSpec((BLK, BLK), lambda g, m: (0, 0))],
        out_specs=pl.BlockSpec((BLK, w), lambda g, m: (m, g)),
        out_shape=jax.ShapeDtypeStruct((LP, GROUP_W), BF16),
        scratch_shapes=[pltpu.VMEM((ATTN_NH, BLK, HEAD_DIM), F32),
                        pltpu.VMEM((ATTN_NH, BLK, V7X_LANES), F32)],
        compiler_params=pltpu.CompilerParams(dimension_semantics=("arbitrary", "arbitrary"),
                                             vmem_limit_bytes=VMEM_LIMIT_BYTES),
        name="sb_attention",
    )(score_bound, qn, kn, vb, p, _suffix_sum_matrix())


def _proj_out_last_kernel(ya_ref, yb_ref, yc_ref, yd_ref, w_ref, h_ref, o_ref):
    y = jnp.concatenate([ya_ref[...], yb_ref[...], yc_ref[...], yd_ref[...]], axis=1)
    o_ref[...] = h_ref[...] + jnp.dot(y, w_ref[...], preferred_element_type=F32)


def _proj_out_kernel(ya_ref, yb_ref, yc_ref, yd_ref, w_ref, h_ref, g_ref, o_ref, hg_ref, ssq_ref):
    @pl.when(pl.program_id(1) == 0)
    def _():
        ssq_ref[...] = jnp.zeros_like(ssq_ref)

    y = jnp.concatenate([ya_ref[...], yb_ref[...], yc_ref[...], yd_ref[...]], axis=1)
    hn = h_ref[...] + jnp.dot(y, w_ref[...], preferred_element_type=F32)
    o_ref[...] = hn
    hg_ref[...] = (hn * g_ref[...]).astype(BF16)
    ssq_ref[...] += jnp.sum(hn * hn, axis=-1, keepdims=True)


def _proj_out(ya, yb, yc, yd, w_all_bf16, layer, hp, g_next):
    yspec = pl.BlockSpec((MM_TM, GROUP_W), lambda i, j: (i, 0))
    tile = pl.BlockSpec((MM_TM, MM_TN), lambda i, j: (i, j))
    wspec = pl.BlockSpec((None, D_MODEL, MM_TN), lambda i, j: (layer, 0, j))
    params = pltpu.CompilerParams(dimension_semantics=("arbitrary", "arbitrary"),
                                  vmem_limit_bytes=VMEM_LIMIT_BYTES)
    if g_next is None:
        def row_of(i):
            return pl.multiple_of(SEQ_ROW0 + i * LAST_TM, BLK)

        yrows = pl.BlockSpec((pl.Element(LAST_TM), pl.Element(GROUP_W)), lambda i, j: (row_of(i), 0))
        hrows = pl.BlockSpec((pl.Element(LAST_TM), pl.Element(MM_TN)),
                             lambda i, j: (row_of(i), pl.multiple_of(j * MM_TN, MM_TN)))
        return pl.pallas_call(
            _proj_out_last_kernel,
            grid=(SEQ // LAST_TM, D_MODEL // MM_TN),
            in_specs=[yrows, yrows, yrows, yrows, wspec, hrows],
            out_specs=pl.BlockSpec((LAST_TM, MM_TN), lambda i, j: (i, j)),
            out_shape=jax.ShapeDtypeStruct((SEQ, D_MODEL), F32),
            compiler_params=params,
            name="proj_out_last",
        )(ya, yb, yc, yd, w_all_bf16, hp)
    return pl.pallas_call(
        _proj_out_kernel,
        grid=(LP // MM_TM, D_MODEL // MM_TN),
        in_specs=[yspec, yspec, yspec, yspec, wspec, tile,
                  pl.BlockSpec((1, MM_TN), lambda i, j: (0, j))],
        out_specs=[tile, tile, pl.BlockSpec((MM_TM, 1), lambda i, j: (i, 0))],
        out_shape=[jax.ShapeDtypeStruct((LP, D_MODEL), F32), jax.ShapeDtypeStruct((LP, D_MODEL), BF16),
                   jax.ShapeDtypeStruct((LP, 1), F32)],
        compiler_params=params,
        name="proj_out",
    )(ya, yb, yc, yd, w_all_bf16, hp, g_next.reshape(1, D_MODEL))


def kernel(x, meta_tokens, norm_g, w_in, conf_dw_w, conf_dw_b, conf_ln_g, conf_ln_b, conf_pw_w,
           sc_conv_w, q_norm_g, k_norm_g, pool_w, pool_scale, w_out):
    assert x.shape == (1, SEQ, D_MODEL)
    hp, hg, ssq = _embed(x.reshape(SEQ, D_MODEL), meta_tokens, norm_g[0])
    w_out_bf16 = w_out.astype(BF16)
    for l in range(DEPTH):
        p = _proj_in(hg, ssq, w_in, l)
        ya, yb, yd, qn, kn, vb = _mixers(p, conf_dw_w[l], conf_dw_b[l], conf_ln_g[l], conf_ln_b[l],
                                         conf_pw_w[l].astype(BF16), sc_conv_w[l], q_norm_g[l],
                                         k_norm_g[l], pool_w[l].astype(BF16), pool_scale[l])
        yc = _attention(qn, kn, vb, p, q_norm_g[l], k_norm_g[l])
        if l + 1 < DEPTH:
            hp, hg, ssq = _proj_out(ya, yb, yc, yd, w_out_bf16, l, hp, norm_g[l + 1])
        else:
            out = _proj_out(ya, yb, yc, yd, w_out_bf16, l, hp, None)
    return out[None]
```

```python
import jax
import jax.numpy as jnp
from jax import lax
from jax.experimental import pallas as pl
from jax.experimental.pallas import tpu as pltpu

D_MODEL = 4096
SEQ = 8192
DEPTH = 4
N_META = 16
GROUP_W = 1024
HEAD_DIM = 128
N_HEADS = GROUP_W // HEAD_DIM
CONF_K = 31
SC_K = 3
POOL_WINDOWS = (2, 4, 8, 16)
POOL_GW = GROUP_W // len(POOL_WINDOWS)
N_CHUNKS = 13
D_IN = N_CHUNKS * GROUP_W
EPS = 1e-6

V7X_LANES = 128
V7X_VMEM_BYTES = 64 * 1024 * 1024
VMEM_LIMIT_BYTES = V7X_VMEM_BYTES - 8 * 1024 * 1024

BLK = 256
ATTN_NH = 8
ATTN_SKEW = 2
LOG2E = 1.4426950408889634
MASKED_SCORE = -1e30
EXP2_ZERO_BELOW = 152.0
L_REAL = N_META + SEQ
LP = -(-L_REAL // BLK) * BLK
PAD = LP - L_REAL
assert (PAD + N_META) % BLK == 0

ROW_TILE = 256
MM_TM = LP // 6
MM_TN = 512
OUT_TM = LP // 8
LAST_TM = 1024
SEQ_ROW0 = LP - SEQ
MIX_TT = 256
CONV_RC = 256
HALO_A = 32
HALO_B = 8
HALO_D = 32
assert SEQ % LAST_TM == 0 and LP % OUT_TM == 0 and OUT_TM % 16 == 0 and LP % MM_TM == 0 and MM_TM % 8 == 0 and LP % MIX_TT == 0 and PAD >= HALO_A

F32 = jnp.float32
BF16 = jnp.bfloat16


def _sigmoid(x):
    return 1.0 / (1.0 + jnp.exp(-x))


def _silu(x):
    return x * _sigmoid(x)


def _embed_kernel(x_ref, meta_ref, g_ref, h_ref, hg_ref, ssq_ref):
    first = jnp.concatenate([jnp.zeros((PAD, D_MODEL), F32), meta_ref[...]], axis=0)
    h = jnp.where(pl.program_id(0) == 0, first, x_ref[...])
    h_ref[...] = h
    hg_ref[...] = (h * g_ref[...]).astype(BF16)
    ssq_ref[...] = jnp.sum(h * h, axis=-1, keepdims=True)


def _embed(x2d, meta, g):
    assert PAD + N_META == ROW_TILE
    row = lambda i: (i, 0)
    return pl.pallas_call(
        _embed_kernel,
        grid=(LP // ROW_TILE,),
        in_specs=[pl.BlockSpec((ROW_TILE, D_MODEL), lambda i: (jnp.maximum(i - 1, 0), 0)),
                  pl.BlockSpec((N_META, D_MODEL), lambda i: (0, 0)),
                  pl.BlockSpec((1, D_MODEL), lambda i: (0, 0))],
        out_specs=[pl.BlockSpec((ROW_TILE, D_MODEL), row), pl.BlockSpec((ROW_TILE, D_MODEL), row),
                   pl.BlockSpec((ROW_TILE, 1), row)],
        out_shape=[jax.ShapeDtypeStruct((LP, D_MODEL), F32), jax.ShapeDtypeStruct((LP, D_MODEL), BF16),
                   jax.ShapeDtypeStruct((LP, 1), F32)],
        compiler_params=pltpu.CompilerParams(dimension_semantics=("arbitrary",),
                                             vmem_limit_bytes=VMEM_LIMIT_BYTES),
        name="embed",
    )(x2d, meta, g.reshape(1, D_MODEL))


def _proj_in_kernel(hg_ref, ssq_ref, w_ref, p_ref):
    acc = jnp.dot(hg_ref[...], w_ref[...].astype(BF16), preferred_element_type=F32)
    p_ref[...] = acc * lax.rsqrt(ssq_ref[...] * (1.0 / D_MODEL) + EPS)


def _proj_in(hg, ssq, w_all, layer):
    return pl.pallas_call(
        _proj_in_kernel,
        grid=(LP // MM_TM, D_IN // MM_TN),
        in_specs=[pl.BlockSpec((MM_TM, D_MODEL), lambda i, j: (i, 0)),
                  pl.BlockSpec((MM_TM, 1), lambda i, j: (i, 0)),
                  pl.BlockSpec((None, D_MODEL, MM_TN), lambda i, j: (layer, 0, j))],
        out_specs=pl.BlockSpec((MM_TM, MM_TN), lambda i, j: (i, j)),
        out_shape=jax.ShapeDtypeStruct((LP, D_IN), F32),
        compiler_params=pltpu.CompilerParams(dimension_semantics=("arbitrary", "arbitrary"),
                                             vmem_limit_bytes=VMEM_LIMIT_BYTES),
        name="proj_in",
    )(hg, ssq, w_all)


def _head_rmsnorm(x, g):
    outs = []
    for hd in range(N_HEADS):
        xh = x[:, hd * HEAD_DIM:(hd + 1) * HEAD_DIM]
        ms = jnp.mean(xh * xh, axis=-1, keepdims=True)
        outs.append(xh * lax.rsqrt(ms + EPS) * g)
    return jnp.concatenate(outs, axis=1)


def _mixer_kernel(ca, cb, cg, sb, sc, sx, sg, q, k, v, pi, pg,
                  ca_h, cb_h, sc_h, sx_h, pi_h,
                  dw_w, dw_b, ln_g, ln_b, pw_w, scw, qg, kg, pool_w, pool_scale,
                  ya_o, yb_o, yd_o, q_o, k_o, v_o,
                  a_ext, s_ext, x_ext, act, a_sh, pool_a, pool_b):
    tt = MIX_TT
    a_ext[0:HALO_A, :] = ca_h[...] * _sigmoid(cb_h[...])
    a_ext[HALO_A:HALO_A + tt, :] = ca[...] * _sigmoid(cb[...])
    n_sh = HALO_A + tt - 8
    for s in range(1, 8):
        a_sh[s - 1] = a_ext[8 - s:8 - s + n_sh, :]
    for r0 in range(0, tt, CONV_RC):
        acc = jnp.broadcast_to(dw_b[...], (CONV_RC, GROUP_W))
        for kk in range(CONF_K):
            back = CONF_K - 1 - kk
            s, base = back % 8, r0 + HALO_A - (back - back % 8)
            if s == 0:
                tap = a_ext[base:base + CONV_RC, :]
            else:
                tap = a_sh[s - 1, base - 8:base - 8 + CONV_RC, :]
            acc = acc + dw_w[kk:kk + 1, :] * tap
        mu = jnp.mean(acc, axis=-1, keepdims=True)
        xc = acc - mu
        var = jnp.mean(xc * xc, axis=-1, keepdims=True)
        y = xc * lax.rsqrt(var + EPS) * ln_g[...] + ln_b[...]
        act[r0:r0 + CONV_RC, :] = _silu(y).astype(BF16)
    ya = jnp.dot(act[...], pw_w[...], preferred_element_type=F32) * _silu(cg[...])
    ya_o[...] = ya.astype(BF16)

    s_ext[0:HALO_B, :] = sc_h[...] * sx_h[...]
    s_ext[HALO_B:HALO_B + tt, :] = sc[...] * sx[...]
    conv = jnp.zeros((tt, GROUP_W), F32)
    for kk in range(SC_K):
        off = HALO_B - (SC_K - 1) + kk
        conv = conv + scw[kk:kk + 1, :] * s_ext[off:off + tt, :]
    yb_o[...] = (sb[...] * conv * _silu(sg[...])).astype(BF16)

    x_ext[0:HALO_D, :] = pi_h[...]
    x_ext[HALO_D:HALO_D + tt, :] = pi[...]
    assert POOL_WINDOWS == (2, 4, 8, 16) and HALO_D == 32
    n_rows, g = HALO_D + tt, POOL_GW
    pool_a[8:n_rows, :] = x_ext[8:n_rows, :] + x_ext[7:n_rows - 1, :]
    pool_b[16:n_rows, g:] = pool_a[16:n_rows, g:] + pool_a[14:n_rows - 2, g:]
    pool_a[24:n_rows, 2 * g:] = pool_b[24:n_rows, 2 * g:] + pool_b[20:n_rows - 4, 2 * g:]
    pool_b[32:n_rows, 3 * g:] = pool_a[32:n_rows, 3 * g:] + pool_a[24:n_rows - 8, 3 * g:]
    window_sums = (pool_a, pool_b, pool_a, pool_b)
    row = pl.program_id(0) * tt + lax.broadcasted_iota(jnp.int32, (tt, 1), 0)
    n_seen = row - (PAD - 1)
    yd_parts = []
    for gi, w in enumerate(POOL_WINDOWS):
        c0 = gi * POOL_GW
        ws = window_sums[gi][HALO_D:HALO_D + tt, c0:c0 + POOL_GW]
        count = jnp.clip(n_seen, 1, w).astype(F32)
        pooled = ws / count - x_ext[HALO_D:HALO_D + tt, c0:c0 + POOL_GW]
        yd_parts.append(jnp.dot(pooled.astype(BF16), pool_w[gi], preferred_element_type=F32))
    yd = jnp.concatenate(yd_parts, axis=1) * pool_scale[...] * _silu(pg[...])
    yd_o[...] = yd.astype(BF16)

    q_o[...] = (_head_rmsnorm(q[...], qg[...]) * (HEAD_DIM ** -0.5 * LOG2E)).astype(BF16)
    k_o[...] = _head_rmsnorm(k[...], kg[...]).astype(BF16)
    v_o[...] = v[...].astype(BF16)


(_CA, _CB, _CG, _SB, _SC, _SX, _SG, _Q, _K, _V, _AG, _PI, _PG) = range(N_CHUNKS)


def _mixers(p, dw_w, dw_b, ln_g, ln_b, pw_w_bf16, scw, qg, kg, pool_w_bf16, pool_scale):
    tt = MIX_TT

    def chunk(c):
        return pl.BlockSpec((tt, GROUP_W), lambda i, c=c: (i, c))

    def halo(c, rows):
        per = tt // rows
        return pl.BlockSpec((rows, GROUP_W), lambda i, c=c, per=per: (jnp.maximum(i * per - 1, 0), c))

    def const(shape):
        nd = len(shape)
        return pl.BlockSpec(shape, lambda i, nd=nd: (0,) * nd)

    main = [_CA, _CB, _CG, _SB, _SC, _SX, _SG, _Q, _K, _V, _PI, _PG]
    in_specs = [chunk(c) for c in main]
    in_specs += [halo(_CA, HALO_A), halo(_CB, HALO_A), halo(_SC, HALO_B), halo(_SX, HALO_B),
                 halo(_PI, HALO_D)]
    in_specs += [const((CONF_K, GROUP_W)), const((1, GROUP_W)), const((1, GROUP_W)), const((1, GROUP_W)),
                 const((GROUP_W, GROUP_W)), const((SC_K, GROUP_W)), const((1, HEAD_DIM)),
                 const((1, HEAD_DIM)), const((len(POOL_WINDOWS), POOL_GW, POOL_GW)),
                 const((1, GROUP_W))]
    out_spec = pl.BlockSpec((tt, GROUP_W), lambda i: (i, 0))
    out_sds = jax.ShapeDtypeStruct((LP, GROUP_W), BF16)
    return pl.pallas_call(
        _mixer_kernel,
        grid=(LP // tt,),
        in_specs=in_specs,
        out_specs=[out_spec] * 6,
        out_shape=[out_sds] * 6,
        scratch_shapes=[pltpu.VMEM((HALO_A + tt, GROUP_W), F32),
                        pltpu.VMEM((HALO_B + tt, GROUP_W), F32),
                        pltpu.VMEM((HALO_D + tt, GROUP_W), F32),
                        pltpu.VMEM((tt, GROUP_W), BF16),
                        pltpu.VMEM((7, HALO_A + tt - 8, GROUP_W), F32),
                        pltpu.VMEM((HALO_D + tt, GROUP_W), F32),
                        pltpu.VMEM((HALO_D + tt, GROUP_W), F32)],
        compiler_params=pltpu.CompilerParams(dimension_semantics=("arbitrary",),
                                             vmem_limit_bytes=VMEM_LIMIT_BYTES),
        name="mixers",
    )(*([p] * 17), dw_w, dw_b.reshape(1, GROUP_W), ln_g.reshape(1, GROUP_W), ln_b.reshape(1, GROUP_W),
      pw_w_bf16, scw, qg.reshape(1, HEAD_DIM), kg.reshape(1, HEAD_DIM), pool_w_bf16,
      pool_scale.reshape(1, GROUP_W))


def _suffix_sum_matrix():
    j = jnp.arange(BLK)[:, None]
    s = jnp.arange(BLK)[None, :]
    return (j >= s).astype(BF16)


def _neg_abs(x):
    bits = lax.bitcast_convert_type(x, jnp.int32) | jnp.int32(-2 ** 31)
    return lax.bitcast_convert_type(bits, F32)


def _attn_kernel(thr_ref, q_ref, k_ref, v_ref, ag_ref, u_ref, o_ref, acc_ref, c_ref):
    m = pl.program_id(1)
    acc_ref[...] = jnp.zeros_like(acc_ref)
    c_ref[...] = jnp.zeros_like(c_ref)
    q = q_ref[...]

    def blocks(js, masked):
        kbs, vbs, valids = [], [], []
        for j in js:
            start = pl.multiple_of(j * BLK, BLK)
            kbs.append(k_ref[pl.ds(start, BLK), :])
            vbs.append(v_ref[pl.ds(start, BLK), :])
            if masked:
                row = m * BLK + lax.broadcasted_iota(jnp.int32, (BLK, BLK), 0)
                col = j * BLK + lax.broadcasted_iota(jnp.int32, (BLK, BLK), 1)
                valids.append(jnp.logical_and(col < row, col >= PAD))
        chains = [(b, hd) for b in range(len(js)) for hd in range(ATTN_NH)]
        u = u_ref[...]
        zs, sps, wts = {}, {}, {}

        def scores(ci):
            b, hd = chains[ci]
            sl = slice(hd * HEAD_DIM, (hd + 1) * HEAD_DIM)
            z = lax.dot_general(q[:, sl], kbs[b][:, sl], (((1,), (1,)), ((), ())),
                                preferred_element_type=F32)
            if masked:
                z = jnp.where(valids[b], z, MASKED_SCORE)
            zs[ci] = z
            sps[ci] = jnp.maximum(z, 0.0) + jnp.log(1.0 + jnp.exp2(_neg_abs(z))) * LOG2E

        def weights(ci):
            _, hd = chains[ci]
            ssum = jnp.dot(sps[ci].astype(BF16), u, preferred_element_type=F32)
            c = c_ref[hd]
            wts[ci] = jnp.exp2(zs[ci] - (ssum + jnp.concatenate([c, c], axis=1)))
            c_ref[hd] = c + jnp.broadcast_to(ssum[:, 0:1], (BLK, V7X_LANES))

        def values(ci):
            b, hd = chains[ci]
            sl = slice(hd * HEAD_DIM, (hd + 1) * HEAD_DIM)
            acc_ref[hd] += jnp.dot(wts[ci].astype(BF16), vbs[b][:, sl], preferred_element_type=F32)

        n = len(chains)
        for t in range(n + 2 * ATTN_SKEW):
            if 0 <= t - 2 * ATTN_SKEW < n:
                values(t - 2 * ATTN_SKEW)
            if 0 <= t - ATTN_SKEW < n:
                weights(t - ATTN_SKEW)
            if t < n:
                scores(t)

    score_bound = thr_ref[0]

    def more_to_do():
        return (jnp.min(c_ref[...]) < score_bound + EXP2_ZERO_BELOW).astype(jnp.int32)

    blocks([m], True)

    def cond(carry):
        jj, live = carry
        return jnp.logical_and(jj < m, live == 1)

    def body(carry):
        jj, _ = carry
        blocks([m - jj], False)
        return jj + 1, more_to_do()

    _, live = lax.while_loop(cond, body, (jnp.int32(1), more_to_do()))

    @pl.when(jnp.logical_and(m > 0, live == 1))
    def _():
        blocks([0], True)

    acc = jnp.concatenate([acc_ref[hd] for hd in range(ATTN_NH)], axis=1)
    o_ref[...] = (acc * _silu(ag_ref[...])).astype(BF16)


def _attention(qn, kn, vb, p, qg, kg):
    score_bound = (1.02 * LOG2E * HEAD_DIM ** 0.5) * jnp.max(jnp.abs(qg)) * jnp.max(jnp.abs(kg))
    score_bound = score_bound.astype(F32).reshape(1)
    nq = LP // BLK
    w = ATTN_NH * HEAD_DIM
    ag_col0 = _AG * (GROUP_W // w)
    return pl.pallas_call(
        _attn_kernel,
        grid=(N_HEADS // ATTN_NH, nq),
        in_specs=[pl.BlockSpec(memory_space=pltpu.SMEM),
                  pl.BlockSpec((BLK, w), lambda g, m: (m, g)),
                  pl.BlockSpec((LP, w), lambda g, m: (0, g), pipeline_mode=pl.Buffered(1)),
                  pl.BlockSpec((LP, w), lambda g, m: (0, g), pipeline_mode=pl.Buffered(1)),
                  pl.BlockSpec((BLK, w), lambda g, m: (m, ag_col0 + g)),
                  pl.BlockSpec((BLK, BLK), lambda g, m: (0, 0))],
        out_specs=pl.BlockSpec((BLK, w), lambda g, m: (m, g)),
        out_shape=jax.ShapeDtypeStruct((LP, GROUP_W), BF16),
        scratch_shapes=[pltpu.VMEM((ATTN_NH, BLK, HEAD_DIM), F32),
                        pltpu.VMEM((ATTN_NH, BLK, V7X_LANES), F32)],
        compiler_params=pltpu.CompilerParams(dimension_semantics=("arbitrary", "arbitrary"),
                                             vmem_limit_bytes=VMEM_LIMIT_BYTES),
        name="sb_attention",
    )(score_bound, qn, kn, vb, p, _suffix_sum_matrix())


def _proj_out_last_kernel(ya_ref, yb_ref, yc_ref, yd_ref, w_ref, h_ref, o_ref):
    y = jnp.concatenate([ya_ref[...], yb_ref[...], yc_ref[...], yd_ref[...]], axis=1)
    o_ref[...] = h_ref[...] + jnp.dot(y, w_ref[...].astype(BF16), preferred_element_type=F32)


def _proj_out_kernel(ya_ref, yb_ref, yc_ref, yd_ref, w_ref, h_ref, g_ref, o_ref, hg_ref, ssq_ref):
    @pl.when(pl.program_id(1) == 0)
    def _():
        ssq_ref[...] = jnp.zeros_like(ssq_ref)

    y = jnp.concatenate([ya_ref[...], yb_ref[...], yc_ref[...], yd_ref[...]], axis=1)
    hn = h_ref[...] + jnp.dot(y, w_ref[...].astype(BF16), preferred_element_type=F32)
    o_ref[...] = hn
    hg_ref[...] = (hn * g_ref[...]).astype(BF16)
    ssq_ref[...] += jnp.sum(hn * hn, axis=-1, keepdims=True)


def _proj_out(ya, yb, yc, yd, w_all, layer, hp, g_next):
    yspec = pl.BlockSpec((OUT_TM, GROUP_W), lambda i, j: (i, 0))
    tile = pl.BlockSpec((OUT_TM, MM_TN), lambda i, j: (i, j))
    wspec = pl.BlockSpec((None, D_MODEL, MM_TN), lambda i, j: (layer, 0, j))
    params = pltpu.CompilerParams(dimension_semantics=("arbitrary", "arbitrary"),
                                  vmem_limit_bytes=VMEM_LIMIT_BYTES)
    if g_next is None:
        def row_of(i):
            return pl.multiple_of(SEQ_ROW0 + i * LAST_TM, BLK)

        yrows = pl.BlockSpec((pl.Element(LAST_TM), pl.Element(GROUP_W)), lambda i, j: (row_of(i), 0))
        hrows = pl.BlockSpec((pl.Element(LAST_TM), pl.Element(MM_TN)),
                             lambda i, j: (row_of(i), pl.multiple_of(j * MM_TN, MM_TN)))
        return pl.pallas_call(
            _proj_out_last_kernel,
            grid=(SEQ // LAST_TM, D_MODEL // MM_TN),
            in_specs=[yrows, yrows, yrows, yrows, wspec, hrows],
            out_specs=pl.BlockSpec((LAST_TM, MM_TN), lambda i, j: (i, j)),
            out_shape=jax.ShapeDtypeStruct((SEQ, D_MODEL), F32),
            compiler_params=params,
            name="proj_out_last",
        )(ya, yb, yc, yd, w_all, hp)
    return pl.pallas_call(
        _proj_out_kernel,
        grid=(LP // OUT_TM, D_MODEL // MM_TN),
        in_specs=[yspec, yspec, yspec, yspec, wspec, tile,
                  pl.BlockSpec((1, MM_TN), lambda i, j: (0, j))],
        out_specs=[tile, tile, pl.BlockSpec((OUT_TM, 1), lambda i, j: (i, 0))],
        out_shape=[jax.ShapeDtypeStruct((LP, D_MODEL), F32), jax.ShapeDtypeStruct((LP, D_MODEL), BF16),
                   jax.ShapeDtypeStruct((LP, 1), F32)],
        compiler_params=params,
        name="proj_out",
    )(ya, yb, yc, yd, w_all, hp, g_next.reshape(1, D_MODEL))


def kernel(x, meta_tokens, norm_g, w_in, conf_dw_w, conf_dw_b, conf_ln_g, conf_ln_b, conf_pw_w,
           sc_conv_w, q_norm_g, k_norm_g, pool_w, pool_scale, w_out):
    assert x.shape == (1, SEQ, D_MODEL)
    hp, hg, ssq = _embed(x.reshape(SEQ, D_MODEL), meta_tokens, norm_g[0])
    for l in range(DEPTH):
        p = _proj_in(hg, ssq, w_in, l)
        ya, yb, yd, qn, kn, vb = _mixers(p, conf_dw_w[l], conf_dw_b[l], conf_ln_g[l], conf_ln_b[l],
                                         conf_pw_w[l].astype(BF16), sc_conv_w[l], q_norm_g[l],
                                         k_norm_g[l], pool_w[l].astype(BF16), pool_scale[l])
        yc = _attention(qn, kn, vb, p, q_norm_g[l], k_norm_g[l])
        if l + 1 < DEPTH:
            hp, hg, ssq = _proj_out(ya, yb, yc, yd, w_out, l, hp, norm_g[l + 1])
        else:
            out = _proj_out(ya, yb, yc, yd, w_out, l, hp, None)
    return out[None]
```
